```python
import math
import jax, jax.numpy as jnp
from jax import lax
import numpy as np

D_MODEL = 2048
BATCH = 4
SEQ = 2048
DEPTH = 2

HEAD_DIM = 128
SB_HEADS = 8
SB_WIDTH = SB_HEADS * HEAD_DIM
MLA_HEADS = 8
MLA_Q_LORA = 512
MLA_KV_LORA = 256
MLA_NOPE = 128
MLA_ROPE = 64
MLA_V = 128
MLA_WIDTH = MLA_HEADS * MLA_V
EVEN_IN_WIDTH = 3 * SB_WIDTH + MLA_Q_LORA + MLA_KV_LORA + MLA_ROPE
EVEN_OUT_WIDTH = SB_WIDTH + MLA_WIDTH
RET_HEADS = 8
RET_QK = 256
RET_V = 512
RET_QK_WIDTH = RET_HEADS * RET_QK
RET_V_WIDTH = RET_HEADS * RET_V
ODD_IN_WIDTH = 2 * RET_QK_WIDTH + 2 * RET_V_WIDTH
N_GROUPS = 4
EXPERTS_PER_GROUP = 8
N_EXPERTS = N_GROUPS * EXPERTS_PER_GROUP
TOP_K_IN_GROUP = 2
EXPERT_HIDDEN = 512
BLOCK = 128
CHUNK = 128
ROPE_BASE = 10000.0
EPS = 1e-6

kernel_name = "hybrid_sbattn_mla_retnet_hmoe_adaln"


def rms_norm(x, g):
    xf = x.astype(jnp.float32)
    y = xf * lax.rsqrt(jnp.mean(xf * xf, axis=-1, keepdims=True) + EPS)
    return (y * g.astype(jnp.float32)).astype(x.dtype)


def rope(x, positions):
    d = x.shape[-1]
    half = d // 2
    inv_freq = jnp.exp(-math.log(ROPE_BASE) * jnp.arange(half, dtype=jnp.float32) / half)
    ang = positions.astype(jnp.float32)[..., None] * inv_freq
    ang = ang.reshape(ang.shape[:2] + (1,) * (x.ndim - 3) + (half,))
    cos, sin = jnp.cos(ang), jnp.sin(ang)
    xf = x.astype(jnp.float32)
    x1, x2 = xf[..., :half], xf[..., half:]
    return jnp.concatenate([x1 * cos - x2 * sin, x1 * sin + x2 * cos], axis=-1).astype(x.dtype)


def to_blocks(t, blk):
    B, S, H, d = t.shape
    return t.reshape(B, S // blk, blk, H, d).transpose(1, 0, 3, 2, 4)


def from_blocks(o):
    NB, B, H, blk, d = o.shape
    return o.transpose(1, 0, 3, 2, 4).reshape(B, NB * blk, H * d)


def adaln(cond, w_mod, b_mod):
    mod = (cond @ w_mod + b_mod)[:, None, :]
    shift, scale, gate = jnp.split(mod, 3, axis=-1)
    return shift, scale, gate


def stick_breaking_attention(q, k, v):
    B, S, H, dh = q.shape
    kh = k.transpose(0, 2, 1, 3)
    vh = v.transpose(0, 2, 1, 3)
    kpos = jnp.arange(S)
    inv_sqrt = dh ** -0.5

    def block(args):
        qb, start = args
        z = jnp.einsum("bhqd,bhkd->bhqk", qb, kh).astype(jnp.float32) * inv_sqrt
        qpos = start + jnp.arange(BLOCK)
        strict = kpos[None, :] < qpos[:, None]
        log_keep = jnp.where(strict, jax.nn.log_sigmoid(-z), 0.0)
        after = lax.cumsum(log_keep, axis=3, reverse=True) - log_keep
        a = jnp.where(strict, jnp.exp(jax.nn.log_sigmoid(z) + after), 0.0)
        return jnp.einsum("bhqk,bhkd->bhqd", a.astype(vh.dtype), vh)

    starts = jnp.arange(S // BLOCK, dtype=jnp.int32) * BLOCK
    o = lax.map(block, (to_blocks(q, BLOCK), starts))
    return from_blocks(o)


def mla_attention(q_nope, q_rope, k_nope, k_rope, v):
    B, S, H, _ = q_nope.shape
    kn = k_nope.transpose(0, 2, 1, 3)
    vh = v.transpose(0, 2, 1, 3)
    kpos = jnp.arange(S)
    scale = (MLA_NOPE + MLA_ROPE) ** -0.5

    def block(args):
        qn, qr, start = args
        z = (jnp.einsum("bhqd,bhkd->bhqk", qn, kn)
             + jnp.einsum("bhqr,bkr->bhqk", qr, k_rope)).astype(jnp.float32) * scale
        qpos = start + jnp.arange(BLOCK)
        causal = kpos[None, :] <= qpos[:, None]
        p = jax.nn.softmax(jnp.where(causal, z, -jnp.inf), axis=-1)
        return jnp.einsum("bhqk,bhkd->bhqd", p.astype(vh.dtype), vh)

    starts = jnp.arange(S // BLOCK, dtype=jnp.int32) * BLOCK
    o = lax.map(block, (to_blocks(q_nope, BLOCK), to_blocks(q_rope, BLOCK), starts))
    return from_blocks(o)


def hybrid_attention(h, positions, w_in, q_norm, w_q_up, kv_norm, w_kv_up, w_out):
    B, S, _ = h.shape
    proj = h @ w_in
    cuts = [SB_WIDTH, 2 * SB_WIDTH, 3 * SB_WIDTH,
            3 * SB_WIDTH + MLA_Q_LORA, 3 * SB_WIDTH + MLA_Q_LORA + MLA_KV_LORA]
    sb_q, sb_k, sb_v, c_q, c_kv, k_rope = jnp.split(proj, cuts, axis=-1)
    o_sb = stick_breaking_attention(sb_q.reshape(B, S, SB_HEADS, HEAD_DIM),
                                    sb_k.reshape(B, S, SB_HEADS, HEAD_DIM),
                                    sb_v.reshape(B, S, SB_HEADS, HEAD_DIM))
    q = (rms_norm(c_q, q_norm) @ w_q_up).reshape(B, S, MLA_HEADS, MLA_NOPE + MLA_ROPE)
    q_nope, q_rope = q[..., :MLA_NOPE], rope(q[..., MLA_NOPE:], positions)
    kv = (rms_norm(c_kv, kv_norm) @ w_kv_up).reshape(B, S, MLA_HEADS, MLA_NOPE + MLA_V)
    k_nope, v = kv[..., :MLA_NOPE], kv[..., MLA_NOPE:]
    k_rope = rope(k_rope, positions)
    o_mla = mla_attention(q_nope, q_rope, k_nope, k_rope, v)
    return jnp.concatenate([o_sb, o_mla], axis=-1) @ w_out


def retention(h, positions, w_in, w_out):
    B, S, _ = h.shape
    proj = h @ w_in
    q, k, v, g = jnp.split(proj, [RET_QK_WIDTH, 2 * RET_QK_WIDTH, 2 * RET_QK_WIDTH + RET_V_WIDTH], axis=-1)
    q = rope(q.reshape(B, S, RET_HEADS, RET_QK), positions)
    k = rope(k.reshape(B, S, RET_HEADS, RET_QK), positions) * (RET_QK ** -0.5)
    v = v.reshape(B, S, RET_HEADS, RET_V)

    log_gamma = jnp.log1p(-jnp.exp2(-5.0 - jnp.arange(RET_HEADS, dtype=jnp.float32)))
    idx = jnp.arange(CHUNK, dtype=jnp.float32)
    rel = idx[:, None] - idx[None, :]
    intra = jnp.where(rel >= 0, jnp.exp(jnp.maximum(rel, 0.0) * log_gamma[:, None, None]), 0.0)
    q_decay = jnp.exp((idx + 1.0) * log_gamma[:, None])
    k_decay = jnp.exp((CHUNK - 1.0 - idx) * log_gamma[:, None])
    chunk_decay = jnp.exp(CHUNK * log_gamma)

    def step(state, inputs):
        qc, kc, vc = (t.astype(jnp.float32) for t in inputs)
        scores = jnp.einsum("bhnd,bhmd->bhnm", qc, kc) * intra
        inner = jnp.einsum("bhnm,bhme->bhne", scores, vc)
        cross = jnp.einsum("bhnd,bhde->bhne", qc, state) * q_decay[None, :, :, None]
        new_state = (state * chunk_decay[None, :, None, None]
                     + jnp.einsum("bhmd,bhme->bhde", kc * k_decay[None, :, :, None], vc))
        return new_state, inner + cross

    state0 = jnp.zeros((B, RET_HEADS, RET_QK, RET_V), jnp.float32)
    _, o = lax.scan(step, state0, (to_blocks(q, CHUNK), to_blocks(k, CHUNK), to_blocks(v, CHUNK)))
    o = o.transpose(1, 0, 3, 2, 4).reshape(B, S, RET_HEADS, RET_V)
    o = o * lax.rsqrt(jnp.mean(o * o, axis=-1, keepdims=True) + EPS)
    o = o.reshape(B, S, RET_V_WIDTH)
    return (jax.nn.silu(g.astype(jnp.float32)) * o).astype(h.dtype) @ w_out


def hierarchical_moe(h, w_group, b_group, w_expert, b_expert, w_gate, w_up, w_down):
    B, S, D = h.shape
    ht = h.reshape(B * S, D)
    g_prob = jax.nn.softmax((ht @ w_group + b_group).astype(jnp.float32), axis=-1)
    g_w, g_idx = lax.top_k(g_prob, 1)
    e_logits = (ht @ w_expert + b_expert).astype(jnp.float32).reshape(-1, N_GROUPS, EXPERTS_PER_GROUP)
    sel = jnp.take_along_axis(e_logits, g_idx[:, :, None], axis=1)[:, 0]
    e_w, e_idx = lax.top_k(jax.nn.softmax(sel, axis=-1), TOP_K_IN_GROUP)
    e_w = e_w / jnp.sum(e_w, axis=-1, keepdims=True)
    weights = g_w * e_w
    flat_idx = g_idx * EXPERTS_PER_GROUP + e_idx
    combine = jnp.sum(jax.nn.one_hot(flat_idx, N_EXPERTS, dtype=jnp.float32) * weights[..., None], axis=1)
    combine = combine.astype(ht.dtype)
    out = jnp.zeros_like(ht)
    for e in range(N_EXPERTS):
        a = jax.nn.silu(ht @ w_gate[e]) * (ht @ w_up[e])
        out = out + combine[:, e:e + 1] * (a @ w_down[e])
    return out.reshape(B, S, D)


def setup_inputs(seed: int = 0) -> dict:
    key = jax.random.key(seed)
    ks = jax.random.split(key, 28)
    n_even = (DEPTH + 1) // 2
    n_odd = DEPTH // 2
    D = D_MODEL
    f32 = jnp.float32

    def nrm(k, shape, scale):
        return jax.random.normal(k, shape, f32) * scale

    x = nrm(ks[0], (BATCH, SEQ, D), 1.0)
    c = nrm(ks[1], (BATCH, D), 1.0)
    offsets = jax.random.randint(ks[2], (BATCH, 1), 0, 4096, dtype=jnp.int32)
    positions = offsets + jnp.arange(SEQ, dtype=jnp.int32)[None, :]
    return {
        "x": x,
        "c": c,
        "positions": positions,
        "w_mod_mix": nrm(ks[3], (DEPTH, D, 3 * D), 0.5 * D ** -0.5),
        "b_mod_mix": nrm(ks[4], (DEPTH, 3 * D), 0.02),
        "norm_mix": 1.0 + nrm(ks[5], (DEPTH, D), 0.05),
        "w_mod_ffn": nrm(ks[6], (DEPTH, D, 3 * D), 0.5 * D ** -0.5),
        "b_mod_ffn": nrm(ks[7], (DEPTH, 3 * D), 0.02),
        "norm_ffn": 1.0 + nrm(ks[8], (DEPTH, D), 0.05),
        "ev_w_in": nrm(ks[9], (n_even, D, EVEN_IN_WIDTH), D ** -0.5),
        "ev_q_norm": 1.0 + nrm(ks[10], (n_even, MLA_Q_LORA), 0.05),
        "ev_w_q_up": nrm(ks[11], (n_even, MLA_Q_LORA, MLA_HEADS * (MLA_NOPE + MLA_ROPE)), MLA_Q_LORA ** -0.5),
        "ev_kv_norm": 1.0 + nrm(ks[12], (n_even, MLA_KV_LORA), 0.05),
        "ev_w_kv_up": nrm(ks[13], (n_even, MLA_KV_LORA, MLA_HEADS * (MLA_NOPE + MLA_V)), MLA_KV_LORA ** -0.5),
        "ev_w_out": nrm(ks[14], (n_even, EVEN_OUT_WIDTH, D), EVEN_OUT_WIDTH ** -0.5),
        "od_w_in": nrm(ks[15], (n_odd, D, ODD_IN_WIDTH), D ** -0.5),
        "od_w_out": nrm(ks[16], (n_odd, RET_V_WIDTH, D), RET_V_WIDTH ** -0.5),
        "moe_w_group": nrm(ks[17], (DEPTH, D, N_GROUPS), D ** -0.5),
        "moe_b_group": nrm(ks[18], (DEPTH, N_GROUPS), 0.01),
        "moe_w_expert": nrm(ks[19], (DEPTH, D, N_EXPERTS), D ** -0.5),
        "moe_b_expert": nrm(ks[20], (DEPTH, N_EXPERTS), 0.01),
        "moe_w_gate": nrm(ks[21], (DEPTH, N_EXPERTS, D, EXPERT_HIDDEN), D ** -0.5),
        "moe_w_up": nrm(ks[22], (DEPTH, N_EXPERTS, D, EXPERT_HIDDEN), D ** -0.5),
        "moe_w_down": nrm(ks[23], (DEPTH, N_EXPERTS, EXPERT_HIDDEN, D), EXPERT_HIDDEN ** -0.5),
        "final_norm": 1.0 + nrm(ks[24], (D,), 0.05),
    }


def reference(x, c, positions, w_mod_mix, b_mod_mix, norm_mix, w_mod_ffn, b_mod_ffn, norm_ffn,
              ev_w_in, ev_q_norm, ev_w_q_up, ev_kv_norm, ev_w_kv_up, ev_w_out,
              od_w_in, od_w_out,
              moe_w_group, moe_b_group, moe_w_expert, moe_b_expert, moe_w_gate, moe_w_up, moe_w_down,
              final_norm):
    cond = jax.nn.silu(c)
    for layer in range(DEPTH):
        i = layer // 2
        shift, scale, gate = adaln(cond, w_mod_mix[layer], b_mod_mix[layer])
        h = rms_norm(x, norm_mix[layer]) * (1.0 + scale) + shift
        if layer % 2 == 0:
            y = hybrid_attention(h, positions, ev_w_in[i], ev_q_norm[i], ev_w_q_up[i],
                                 ev_kv_norm[i], ev_w_kv_up[i], ev_w_out[i])
        else:
            y = retention(h, positions, od_w_in[i], od_w_out[i])
        x = x + gate * y
        shift, scale, gate = adaln(cond, w_mod_ffn[layer], b_mod_ffn[layer])
        h = rms_norm(x, norm_ffn[layer]) * (1.0 + scale) + shift
        x = x + gate * hierarchical_moe(h, moe_w_group[layer], moe_b_group[layer],
                                        moe_w_expert[layer], moe_b_expert[layer],
                                        moe_w_gate[layer], moe_w_up[layer], moe_w_down[layer])
    return rms_norm(x, final_norm)
```

```python
import functools
import math

import jax
import jax.numpy as jnp
from jax import lax
from jax.experimental import pallas as pl
from jax.experimental.pallas import tpu as pltpu

F32 = jnp.float32
BF16 = jnp.bfloat16
I32 = jnp.int32

D_MODEL = 2048
BATCH = 4
SEQ = 2048
DEPTH = 2
TOKENS = BATCH * SEQ

HEAD_DIM = 128
SB_HEADS = 8
SB_WIDTH = SB_HEADS * HEAD_DIM
MLA_HEADS = 8
MLA_Q_LORA = 512
MLA_KV_LORA = 256
MLA_NOPE = 128
MLA_ROPE = 64
MLA_V = 128
MLA_QK_PAD = 256
EVEN_IN_PAD = 4096
RET_HEADS = 8
RET_QK = 256
RET_V = 512
RET_QK_WIDTH = RET_HEADS * RET_QK
RET_V_WIDTH = RET_HEADS * RET_V
ODD_IN_WIDTH = 2 * RET_QK_WIDTH + 2 * RET_V_WIDTH
N_GROUPS = 4
EXPERTS_PER_GROUP = 8
N_EXPERTS = N_GROUPS * EXPERTS_PER_GROUP
EXPERT_HIDDEN = 512
CHUNK = 128
ROPE_BASE = 10000.0
EPS = 1e-6

LANES = 128
SUBLANES = 8
ROW_SLABS = D_MODEL // (SUBLANES * LANES)
ROW_CHUNKS = D_MODEL // LANES

ROUTER_ROWS = 48
MOE_TM = 256
MOE_TILES = (2 * TOKENS) // MOE_TM + N_EXPERTS
MOE_ROWS = (MOE_TILES + 1) * MOE_TM

NT_DIMS = (((1,), (1,)), ((), ()))


def _cparams(n_grid, vmem_mb):
    return pltpu.CompilerParams(dimension_semantics=("arbitrary",) * n_grid,
                                vmem_limit_bytes=vmem_mb * 1024 * 1024)


def _norm_mod(x, g, scale, shift):
    y = x * lax.rsqrt(jnp.mean(x * x, axis=-1, keepdims=True) + EPS)
    return (y * g) * (1.0 + scale) + shift


def _to_row_tiles(ref_at, val, rows):
    for c in range(ROW_CHUNKS):
        a, b = divmod(c, SUBLANES)
        ref_at[a, pl.ds(b, rows, stride=SUBLANES), :] = val[:, c * LANES:(c + 1) * LANES]


def _from_row_tiles(ref_at, rows):
    parts = []
    for c in range(ROW_CHUNKS):
        a, b = divmod(c, SUBLANES)
        parts.append(ref_at[a, pl.ds(b, rows, stride=SUBLANES), :])
    return jnp.concatenate(parts, axis=1)


def _mods_kernel(c_ref, w_ref, b_ref, o_ref, cond_ref):
    @pl.when((pl.program_id(0) == 0) & (pl.program_id(1) == 0))
    def _():
        c = c_ref[...]
        cond_ref[...] = c * jax.nn.sigmoid(c)

    tn = w_ref.shape[2]
    rows = []
    for b in range(BATCH):
        cb = cond_ref[b]
        cols = [jnp.sum(w_ref[0, :, j * LANES:(j + 1) * LANES] * cb, axis=0, keepdims=True)
                for j in range(tn // LANES)]
        rows.append(jnp.concatenate(cols, axis=1))
    o_ref[0] = jnp.concatenate(rows, axis=0) + b_ref[0]


def _mods(c_lanes, w_mod, b_mod):
    tn = 512
    n3 = 3 * D_MODEL
    return pl.pallas_call(
        _mods_kernel,
        grid=(DEPTH, n3 // tn),
        in_specs=[pl.BlockSpec((BATCH, D_MODEL, LANES), lambda l, j: (0, 0, 0)),
                  pl.BlockSpec((1, D_MODEL, tn), lambda l, j: (l, 0, j)),
                  pl.BlockSpec((1, 1, tn), lambda l, j: (l, 0, j))],
        out_specs=pl.BlockSpec((1, BATCH, tn), lambda l, j: (l, 0, j)),
        out_shape=jax.ShapeDtypeStruct((DEPTH, BATCH, n3), F32),
        scratch_shapes=[pltpu.VMEM((BATCH, D_MODEL, LANES), F32)],
        compiler_params=_cparams(2, 32),
        name="adaln_mods",
    )(c_lanes, w_mod, b_mod.reshape(DEPTH, 1, n3))


def _rope_tables_kernel(pos_ref, fm_ref, sg_ref, fr_ref, cm_ref, sm_ref, cr_ref, sr_ref):
    p = pos_ref[...].astype(F32)
    am = p * fm_ref[...]
    cm_ref[...] = jnp.cos(am)
    sm_ref[...] = jnp.sin(am) * sg_ref[...]
    ar = p * fr_ref[...]
    cr_ref[...] = jnp.cos(ar)
    sr_ref[...] = jnp.sin(ar)


def _rope_tables(pos_lanes):
    half = MLA_ROPE // 2
    f_mla = jnp.exp(-math.log(ROPE_BASE) * jnp.arange(half, dtype=F32) / half)
    z = jnp.zeros((half,), F32)
    fm = jnp.concatenate([f_mla, z, f_mla, z]).reshape(1, LANES)
    sg = jnp.concatenate([-jnp.ones((half,), F32), z, jnp.ones((half,), F32), z]).reshape(1, LANES)
    hr = RET_QK // 2
    fr = jnp.exp(-math.log(ROPE_BASE) * jnp.arange(hr, dtype=F32) / hr).reshape(1, LANES)
    tm = 1024
    row = pl.BlockSpec((tm, LANES), lambda i: (i, 0))
    vec = pl.BlockSpec((1, LANES), lambda i: (0, 0))
    tab = jax.ShapeDtypeStruct((TOKENS, LANES), F32)
    return pl.pallas_call(
        _rope_tables_kernel,
        grid=(TOKENS // tm,),
        in_specs=[row, vec, vec, vec],
        out_specs=[row, row, row, row],
        out_shape=[tab, tab, tab, tab],
        compiler_params=_cparams(1, 32),
        name="rope_tables",
    )(pos_lanes, fm, sg, fr)


def _in_proj_kernel(x_ref, g_ref, sc_ref, sh_ref, w_ref, o_ref, h_ref):
    @pl.when(pl.program_id(1) == 0)
    def _():
        h_ref[...] = _norm_mod(x_ref[...], g_ref[...], sc_ref[0], sh_ref[0]).astype(BF16)

    o_ref[...] = jnp.dot(h_ref[...], w_ref[...].astype(BF16),
                         preferred_element_type=F32).astype(o_ref.dtype)


def _in_proj(x2d, g, scale, shift, w):
    tm, tn = 1024, 512
    n = w.shape[1]
    per_batch = SEQ // tm
    mod = pl.BlockSpec((1, 1, D_MODEL), lambda i, j: (i // per_batch, 0, 0))
    return pl.pallas_call(
        _in_proj_kernel,
        grid=(TOKENS // tm, n // tn),
        in_specs=[pl.BlockSpec((tm, D_MODEL), lambda i, j: (i, 0)),
                  pl.BlockSpec((1, D_MODEL), lambda i, j: (0, 0)),
                  mod, mod,
                  pl.BlockSpec((D_MODEL, tn), lambda i, j: (0, j))],
        out_specs=pl.BlockSpec((tm, tn), lambda i, j: (i, j)),
        out_shape=jax.ShapeDtypeStruct((TOKENS, n), BF16),
        scratch_shapes=[pltpu.VMEM((tm, D_MODEL), BF16)],
        compiler_params=_cparams(2, 48),
        name="in_proj",
    )(x2d, g, scale, shift, w)


def _mla_prep_kernel(cq_ref, ckv_ref, kr_ref, qn_ref, kvn_ref, wq_ref, wkv_ref, cos_ref, sin_ref,
                     q_ref, k_ref, v_ref):
    cos = cos_ref[...]
    sin = sin_ref[...]

    def rope(x):
        return x * cos + pltpu.roll(x, LANES // 2, 1) * sin

    def rms(x, g):
        xf = x.astype(F32)
        return (xf * lax.rsqrt(jnp.mean(xf * xf, axis=-1, keepdims=True) + EPS)) * g

    cq = rms(cq_ref[...], qn_ref[...]).astype(BF16)
    q = jnp.dot(cq, wq_ref[...].astype(BF16), preferred_element_type=F32)
    ckv = rms(ckv_ref[...], kvn_ref[...]).astype(BF16)
    kv = jnp.dot(ckv, wkv_ref[...].astype(BF16), preferred_element_type=F32)
    kr = rope(kr_ref[...].astype(F32)).astype(BF16)
    scale = (MLA_NOPE + MLA_ROPE) ** -0.5
    for h in range(MLA_HEADS):
        c0 = h * MLA_QK_PAD
        q_ref[:, c0:c0 + MLA_NOPE] = (q[:, c0:c0 + MLA_NOPE] * scale).astype(BF16)
        q_ref[:, c0 + MLA_NOPE:c0 + MLA_QK_PAD] = (
            rope(q[:, c0 + MLA_NOPE:c0 + MLA_QK_PAD]) * scale).astype(BF16)
        k_ref[:, c0:c0 + MLA_NOPE] = kv[:, h * MLA_NOPE:(h + 1) * MLA_NOPE].astype(BF16)
        k_ref[:, c0 + MLA_NOPE:c0 + MLA_QK_PAD] = kr
    v_ref[...] = kv[:, MLA_HEADS * MLA_NOPE:].astype(BF16)


def _mla_prep(proj, q_norm, kv_norm, wq, wkv, cos_m, sin_m):
    tm = 512
    c_q0 = 3 * SB_WIDTH
    c_kv0 = c_q0 + MLA_Q_LORA
    c_kr0 = c_kv0 + MLA_KV_LORA
    full = lambda shape: pl.BlockSpec(shape, lambda i: (0, 0))
    row = lambda w: pl.BlockSpec((tm, w), lambda i: (i, 0))
    qk_w = MLA_HEADS * MLA_QK_PAD
    return pl.pallas_call(
        _mla_prep_kernel,
        grid=(TOKENS // tm,),
        in_specs=[pl.BlockSpec((tm, MLA_Q_LORA), lambda i: (i, c_q0 // MLA_Q_LORA)),
                  pl.BlockSpec((tm, MLA_KV_LORA), lambda i: (i, c_kv0 // MLA_KV_LORA)),
                  pl.BlockSpec((tm, LANES), lambda i: (i, c_kr0 // LANES)),
                  full((1, MLA_Q_LORA)), full((1, MLA_KV_LORA)),
                  full((MLA_Q_LORA, qk_w)), full((MLA_KV_LORA, 2 * MLA_HEADS * MLA_NOPE)),
                  row(LANES), row(LANES)],
        out_specs=[row(qk_w), row(qk_w), row(MLA_HEADS * MLA_V)],
        out_shape=[jax.ShapeDtypeStruct((TOKENS, qk_w), BF16),
                   jax.ShapeDtypeStruct((TOKENS, qk_w), BF16),
                   jax.ShapeDtypeStruct((TOKENS, MLA_HEADS * MLA_V), BF16)],
        compiler_params=_cparams(1, 48),
        name="mla_prep",
    )(proj, proj, proj, q_norm, kv_norm, wq, wkv, cos_m, sin_m)


ATT_T = 256


def _sb_attn_kernel(q_ref, k_ref, v_ref, o_ref):
    t = ATT_T
    ii = lax.broadcasted_iota(I32, (t, t), 0)
    jj = lax.broadcasted_iota(I32, (t, t), 1)
    strict = jj < ii
    upper = (ii > jj).astype(BF16)
    inv_sqrt = HEAD_DIM ** -0.5

    def block(q, kb, run, acc, diag):
        rows = pl.ds(pl.multiple_of(kb * t, t), t)
        k = k_ref[rows, :]
        v = v_ref[rows, :]
        z = lax.dot_general(q, k, NT_DIMS, preferred_element_type=F32) * inv_sqrt
        log_beta = jnp.minimum(z, 0.0) - jnp.log1p(jnp.exp(-jnp.abs(z)))
        log_keep = log_beta - z
        if diag:
            log_keep = jnp.where(strict, log_keep, 0.0)
        hi = log_keep.astype(BF16)
        lo = (log_keep - hi.astype(F32)).astype(BF16)
        after = (jnp.dot(hi, upper, preferred_element_type=F32)
                 + jnp.dot(lo, upper, preferred_element_type=F32)) + run
        a = jnp.exp(log_beta + after)
        if diag:
            a = jnp.where(strict, a, 0.0)
        acc = acc + jnp.dot(a.astype(BF16), v, preferred_element_type=F32)
        run = run + jnp.sum(log_keep, axis=1, keepdims=True)
        return run, acc

    def q_loop(qi, carry):
        qrows = pl.ds(pl.multiple_of(qi * t, t), t)
        q = q_ref[qrows, :]
        run, acc = block(q, qi, jnp.zeros((t, 1), F32), jnp.zeros((t, HEAD_DIM), F32), True)

        def k_loop(it, c):
            return block(q, qi - 1 - it, c[0], c[1], False)

        run, acc = lax.fori_loop(0, qi, k_loop, (run, acc))
        o_ref[qrows, :] = acc.astype(o_ref.dtype)
        return carry

    lax.fori_loop(0, SEQ // t, q_loop, 0)


def _sb_attention(proj):
    blk = lambda off: pl.BlockSpec((SEQ, HEAD_DIM), lambda b, h: (b, off + h))
    return pl.pallas_call(
        _sb_attn_kernel,
        grid=(BATCH, SB_HEADS),
        in_specs=[blk(0), blk(SB_HEADS), blk(2 * SB_HEADS)],
        out_specs=pl.BlockSpec((SEQ, HEAD_DIM), lambda b, h: (b, h)),
        out_shape=jax.ShapeDtypeStruct((TOKENS, SB_WIDTH), BF16),
        compiler_params=_cparams(2, 48),
        name="sb_attention",
    )(proj, proj, proj)


def _mla_attn_kernel(q_ref, k_ref, v_ref, o_ref):
    t = ATT_T
    ii = lax.broadcasted_iota(I32, (t, t), 0)
    jj = lax.broadcasted_iota(I32, (t, t), 1)
    causal = jj <= ii
    neg = -1e30

    def step(q, kb, carry, diag):
        m, l, acc = carry
        rows = pl.ds(pl.multiple_of(kb * t, t), t)
        s = lax.dot_general(q, k_ref[rows, :], NT_DIMS, preferred_element_type=F32)
        if diag:
            s = jnp.where(causal, s, neg)
        m_new = jnp.maximum(m, jnp.max(s, axis=1, keepdims=True))
        alpha = jnp.exp(m - m_new)
        p = jnp.exp(s - m_new)
        l = alpha * l + jnp.sum(p, axis=1, keepdims=True)
        acc = alpha * acc + jnp.dot(p.astype(BF16), v_ref[rows, :], preferred_element_type=F32)
        return m_new, l, acc

    def q_loop(qi, carry):
        qrows = pl.ds(pl.multiple_of(qi * t, t), t)
        q = q_ref[qrows, :]
        init = (jnp.full((t, 1), neg, F32), jnp.zeros((t, 1), F32), jnp.zeros((t, MLA_V), F32))
        c = lax.fori_loop(0, qi, lambda kb, c: step(q, kb, c, False), init)
        _, l, acc = step(q, qi, c, True)
        o_ref[qrows, :] = (acc / l).astype(o_ref.dtype)
        return carry

    lax.fori_loop(0, SEQ // t, q_loop, 0)


def _mla_attention(q_cat, k_cat, v):
    qk = pl.BlockSpec((SEQ, MLA_QK_PAD), lambda b, h: (b, h))
    vo = pl.BlockSpec((SEQ, MLA_V), lambda b, h: (b, h))
    return pl.pallas_call(
        _mla_attn_kernel,
        grid=(BATCH, MLA_HEADS),
        in_specs=[qk, qk, vo],
        out_specs=vo,
        out_shape=jax.ShapeDtypeStruct((TOKENS, MLA_HEADS * MLA_V), BF16),
        compiler_params=_cparams(2, 48),
        name="mla_attention",
    )(q_cat, k_cat, v)


def _out_proj_kernel(*refs, n_in):
    y_refs = refs[:n_in]
    w_ref, x_ref, gate_ref, o_ref = refs[n_in:]
    acc = None
    off = 0
    for y_ref in y_refs:
        kk = y_ref.shape[1]
        d = jnp.dot(y_ref[...], w_ref[off:off + kk, :].astype(BF16), preferred_element_type=F32)
        acc = d if acc is None else acc + d
        off += kk
    o_ref[...] = x_ref[...] + gate_ref[0] * acc


def _out_proj(ys, w, x2d, gate):
    tm, tn = 1024, 512
    per_batch = SEQ // tm
    in_specs = [pl.BlockSpec((tm, y.shape[1]), lambda i, j: (i, 0)) for y in ys]
    in_specs += [pl.BlockSpec((w.shape[0], tn), lambda i, j: (0, j)),
                 pl.BlockSpec((tm, tn), lambda i, j: (i, j)),
                 pl.BlockSpec((1, 1, tn), lambda i, j: (i // per_batch, 0, j))]
    return pl.pallas_call(
        functools.partial(_out_proj_kernel, n_in=len(ys)),
        grid=(TOKENS // tm, D_MODEL // tn),
        in_specs=in_specs,
        out_specs=pl.BlockSpec((tm, tn), lambda i, j: (i, j)),
        out_shape=jax.ShapeDtypeStruct((TOKENS, D_MODEL), F32),
        compiler_params=_cparams(2, 56),
        name="out_proj",
    )(*ys, w, x2d, gate)


def _retention_kernel(lg_ref, q_ref, k_ref, v_ref, g_ref, cos_ref, sin_ref, o_ref, state_ref):
    lg = lg_ref[pl.program_id(1)]
    c = CHUNK
    half = RET_QK // 2
    ii = lax.broadcasted_iota(I32, (c, c), 0)
    jj = lax.broadcasted_iota(I32, (c, c), 1)
    rel = (ii - jj).astype(F32)
    intra = jnp.where(rel >= 0, jnp.exp(jnp.maximum(rel, 0.0) * lg), 0.0)
    idx = lax.broadcasted_iota(I32, (c, 1), 0).astype(F32)
    q_decay = jnp.exp((idx + 1.0) * lg)
    k_decay = jnp.exp((c - 1.0 - idx) * lg)
    chunk_decay = jnp.exp(jnp.full((1, 1), float(c), F32) * lg)
    state_ref[...] = jnp.zeros_like(state_ref)

    def rope(x, cos, sin):
        x1, x2 = x[:, :half], x[:, half:]
        return jnp.concatenate([x1 * cos - x2 * sin, x1 * sin + x2 * cos], axis=1)

    def chunk(ci, carry):
        rows = pl.ds(pl.multiple_of(ci * c, c), c)
        cos = cos_ref[rows, :]
        sin = sin_ref[rows, :]
        qr = rope(q_ref[rows, :].astype(F32), cos, sin)
        kr = rope(k_ref[rows, :].astype(F32), cos, sin) * (RET_QK ** -0.5)
        v = v_ref[rows, :]
        qb = qr.astype(BF16)
        scores = lax.dot_general(qb, kr.astype(BF16), NT_DIMS, preferred_element_type=F32) * intra
        inner = jnp.dot(scores.astype(BF16), v, preferred_element_type=F32)
        st = state_ref[...]
        cross = jnp.dot(qb, st.astype(BF16), preferred_element_type=F32) * q_decay
        kd_t = jnp.transpose(kr * k_decay).astype(BF16)
        state_ref[...] = st * chunk_decay + jnp.dot(kd_t, v, preferred_element_type=F32)
        o = inner + cross
        o = o * lax.rsqrt(jnp.mean(o * o, axis=-1, keepdims=True) + EPS)
        gg = g_ref[rows, :].astype(F32)
        o_ref[rows, :] = ((gg * jax.nn.sigmoid(gg)) * o).astype(o_ref.dtype)
        return carry

    lax.fori_loop(0, SEQ // c, chunk, 0)


def _retention(proj, cos_r, sin_r):
    log_gamma = jnp.log1p(-jnp.exp2(-5.0 - jnp.arange(RET_HEADS, dtype=F32)))
    k0 = RET_QK_WIDTH // RET_QK
    v0 = 2 * RET_QK_WIDTH // RET_V
    g0 = (2 * RET_QK_WIDTH + RET_V_WIDTH) // RET_V
    tab = pl.BlockSpec((SEQ, LANES), lambda b, h: (b, 0))
    return pl.pallas_call(
        _retention_kernel,
        grid=(BATCH, RET_HEADS),
        in_specs=[pl.BlockSpec(memory_space=pltpu.SMEM),
                  pl.BlockSpec((SEQ, RET_QK), lambda b, h: (b, h)),
                  pl.BlockSpec((SEQ, RET_QK), lambda b, h: (b, k0 + h)),
                  pl.BlockSpec((SEQ, RET_V), lambda b, h: (b, v0 + h)),
                  pl.BlockSpec((SEQ, RET_V), lambda b, h: (b, g0 + h)),
                  tab, tab],
        out_specs=pl.BlockSpec((SEQ, RET_V), lambda b, h: (b, h)),
        out_shape=jax.ShapeDtypeStruct((TOKENS, RET_V_WIDTH), BF16),
        scratch_shapes=[pltpu.VMEM((RET_QK, RET_V), F32)],
        compiler_params=_cparams(2, 48),
        name="retention",
    )(log_gamma, proj, proj, proj, proj, cos_r, sin_r)


ROUTER_TM = 512


def _router_kernel(x_ref, g_ref, sc_ref, sh_ref, w_ref, b_ref, ei_ref, wcol_ref, cnt_ref, carry_ref):
    tm = ROUTER_TM

    @pl.when(pl.program_id(0) == 0)
    def _():
        carry_ref[...] = jnp.zeros_like(carry_ref)

    def split(a):
        hi = a.astype(BF16)
        return hi, (a - hi.astype(F32)).astype(BF16)

    h = _norm_mod(x_ref[...], g_ref[...], sc_ref[0], sh_ref[0])
    h_hi, h_lo = split(h)
    w_hi, w_lo = split(w_ref[...])
    nt = lambda a, b: lax.dot_general(a, b, NT_DIMS, preferred_element_type=F32)
    logits = (nt(w_hi, h_hi) + nt(w_hi, h_lo) + nt(w_lo, h_hi)) + b_ref[...]
    e_log = logits[0:N_EXPERTS]
    g_log = logits[N_EXPERTS:N_EXPERTS + N_GROUPS]

    def top1(vals, n):
        rows = lax.broadcasted_iota(I32, (n, tm), 0)
        m = jnp.max(vals, axis=0, keepdims=True)
        return m, jnp.min(jnp.where(vals == m, rows, n), axis=0, keepdims=True), rows

    g_max, g_idx, _ = top1(g_log, N_GROUPS)
    g_w = 1.0 / jnp.sum(jnp.exp(g_log - g_max), axis=0, keepdims=True)
    sel = e_log[0:EXPERTS_PER_GROUP]
    for gi in range(1, N_GROUPS):
        sel = jnp.where(g_idx == gi, e_log[gi * EXPERTS_PER_GROUP:(gi + 1) * EXPERTS_PER_GROUP], sel)
    m1, i1, rows8 = top1(sel, EXPERTS_PER_GROUP)
    m2, i2, _ = top1(jnp.where(rows8 == i1, -jnp.inf, sel), EXPERTS_PER_GROUP)
    ratio = jnp.exp(m2 - m1)
    w1 = g_w / (1.0 + ratio)
    w2 = (g_w * ratio) / (1.0 + ratio)
    e1 = g_idx * EXPERTS_PER_GROUP + i1
    e2 = g_idx * EXPERTS_PER_GROUP + i2

    rows32 = lax.broadcasted_iota(I32, (N_EXPERTS, tm), 0)
    hit1 = rows32 == e1
    hit2 = rows32 == e2
    onehot = jnp.where(hit1 | hit2, 1.0, 0.0)
    jj = lax.broadcasted_iota(I32, (tm, tm), 0)
    tt = lax.broadcasted_iota(I32, (tm, tm), 1)
    before = (jj < tt).astype(BF16)
    rank_e = jnp.dot(onehot.astype(BF16), before, preferred_element_type=F32) + carry_ref[:, 0:1]
    r1 = jnp.sum(jnp.where(hit1, rank_e, 0.0), axis=0, keepdims=True)
    r2 = jnp.sum(jnp.where(hit2, rank_e, 0.0), axis=0, keepdims=True)
    carry_ref[...] = carry_ref[...] + jnp.sum(onehot, axis=1, keepdims=True)
    cnt_ref[...] = carry_ref[...]

    ei_ref[...] = jnp.where(rows8 == 0, e1, jnp.where(rows8 == 1, e2, jnp.where(
        rows8 == 2, r1.astype(I32), jnp.where(rows8 == 3, r2.astype(I32), 0))))
    rows128 = lax.broadcasted_iota(I32, (LANES, tm), 0)
    wrows = jnp.where(rows128 == 0, w1, jnp.where(rows128 == 1, w2, 0.0))
    wcol_ref[...] = jnp.transpose(wrows)


def _router(x2d, g, scale, shift, w_t, b_rows):
    tm = ROUTER_TM
    per_batch = SEQ // tm
    mod = pl.BlockSpec((1, 1, D_MODEL), lambda i: (i // per_batch, 0, 0))
    return pl.pallas_call(
        _router_kernel,
        grid=(TOKENS // tm,),
        in_specs=[pl.BlockSpec((tm, D_MODEL), lambda i: (i, 0)),
                  pl.BlockSpec((1, D_MODEL), lambda i: (0, 0)),
                  mod, mod,
                  pl.BlockSpec((ROUTER_ROWS, D_MODEL), lambda i: (0, 0)),
                  pl.BlockSpec((ROUTER_ROWS, tm), lambda i: (0, 0))],
        out_specs=[pl.BlockSpec((SUBLANES, tm), lambda i: (0, i)),
                   pl.BlockSpec((tm, LANES), lambda i: (i, 0)),
                   pl.BlockSpec((N_EXPERTS, LANES), lambda i: (0, 0))],
        out_shape=[jax.ShapeDtypeStruct((SUBLANES, TOKENS), I32),
                   jax.ShapeDtypeStruct((TOKENS, LANES), F32),
                   jax.ShapeDtypeStruct((N_EXPERTS, LANES), F32)],
        scratch_shapes=[pltpu.VMEM((N_EXPERTS, LANES), F32)],
        compiler_params=_cparams(1, 48),
        name="moe_router",
    )(x2d, g, scale, shift, w_t, b_rows)


DISPATCH_TM = 256


def _row_copy(src, dst, sem):
    return pltpu.make_async_copy(src, dst, sem)


def _dispatch_kernel(pos0_ref, pos1_ref, x_ref, g_ref, sc_ref, sh_ref, xs_ref, buf_ref, sem):
    tm = DISPATCH_TM
    i = pl.program_id(0)
    n = pl.num_programs(0)
    slot = i % 2

    def wait_slot(s):
        for _ in range(2):
            _row_copy(buf_ref.at[s], xs_ref.at[:, pl.ds(0, tm * SUBLANES), :], sem.at[s]).wait()

    @pl.when(i >= 2)
    def _():
        wait_slot(slot)

    h = _norm_mod(x_ref[...], g_ref[...], sc_ref[0], sh_ref[0])
    _to_row_tiles(buf_ref.at[slot], h, tm)

    def issue(r, carry):
        tok = i * tm + r
        src = buf_ref.at[slot, :, pl.ds(pl.multiple_of(r * SUBLANES, SUBLANES), SUBLANES), :]
        for pos_ref in (pos0_ref, pos1_ref):
            row = pl.multiple_of(pos_ref[tok] * SUBLANES, SUBLANES)
            _row_copy(src, xs_ref.at[:, pl.ds(row, SUBLANES), :], sem.at[slot]).start()
        return carry

    lax.fori_loop(0, tm, issue, 0)

    @pl.when(i == n - 1)
    def _():
        wait_slot(1 - slot)
        wait_slot(slot)


def _dispatch(pos0, pos1, x2d, g, scale, shift):
    tm = DISPATCH_TM
    per_batch = SEQ // tm
    mod = pl.BlockSpec((1, 1, D_MODEL), lambda i, p0, p1: (i // per_batch, 0, 0))
    grid_spec = pltpu.PrefetchScalarGridSpec(
        num_scalar_prefetch=2,
        grid=(TOKENS // tm,),
        in_specs=[pl.BlockSpec((tm, D_MODEL), lambda i, p0, p1: (i, 0)),
                  pl.BlockSpec((1, D_MODEL), lambda i, p0, p1: (0, 0)),
                  mod, mod],
        out_specs=pl.BlockSpec(memory_space=pl.ANY),
        scratch_shapes=[pltpu.VMEM((2, ROW_SLABS, tm * SUBLANES, LANES), F32),
                        pltpu.SemaphoreType.DMA((2,))])
    return pl.pallas_call(
        _dispatch_kernel,
        grid_spec=grid_spec,
        out_shape=jax.ShapeDtypeStruct((ROW_SLABS, MOE_ROWS * SUBLANES, LANES), F32),
        compiler_params=_cparams(1, 48),
        name="moe_dispatch",
    )(pos0, pos1, x2d, g, scale, shift)


def _experts_kernel(te_ref, tf_ref, tv_ref, ti_ref, to_ref, xs_ref, wg_ref, wu_ref, wd_ref, ys_ref,
                    wgb_ref, wub_ref, wdb_ref):
    del te_ref, ti_ref, to_ref
    tm = MOE_TM
    j = pl.program_id(0)

    @pl.when(tf_ref[j] == 1)
    def _():
        wgb_ref[...] = wg_ref[0, 0].astype(BF16)
        wub_ref[...] = wu_ref[0, 0].astype(BF16)
        wdb_ref[...] = wd_ref[0, 0].astype(BF16)

    @pl.when(tv_ref[j] == 1)
    def _():
        x = _from_row_tiles(xs_ref, tm).astype(BF16)
        gate = jnp.dot(x, wgb_ref[...], preferred_element_type=F32)
        up = jnp.dot(x, wub_ref[...], preferred_element_type=F32)
        a = ((gate * jax.nn.sigmoid(gate)) * up).astype(BF16)
        _to_row_tiles(ys_ref, jnp.dot(a, wdb_ref[...], preferred_element_type=F32), tm)

    @pl.when(tv_ref[j] == 0)
    def _():
        ys_ref[...] = jnp.zeros_like(ys_ref)


def _experts(plan, xs, w_gate, w_up, w_down, layer):
    tm = MOE_TM
    rows = pl.BlockSpec((ROW_SLABS, tm * SUBLANES, LANES), lambda j, te, tf, tv, ti, to: (0, ti[j], 0))
    rows_out = pl.BlockSpec((ROW_SLABS, tm * SUBLANES, LANES), lambda j, te, tf, tv, ti, to: (0, to[j], 0))
    w_in = pl.BlockSpec((1, 1, D_MODEL, EXPERT_HIDDEN), lambda j, te, tf, tv, ti, to: (layer, te[j], 0, 0))
    w_out = pl.BlockSpec((1, 1, EXPERT_HIDDEN, D_MODEL), lambda j, te, tf, tv, ti, to: (layer, te[j], 0, 0))
    grid_spec = pltpu.PrefetchScalarGridSpec(
        num_scalar_prefetch=5,
        grid=(MOE_TILES,),
        in_specs=[rows, w_in, w_in, w_out],
        out_specs=rows_out,
        scratch_shapes=[pltpu.VMEM((D_MODEL, EXPERT_HIDDEN), BF16),
                        pltpu.VMEM((D_MODEL, EXPERT_HIDDEN), BF16),
                        pltpu.VMEM((EXPERT_HIDDEN, D_MODEL), BF16)])
    return pl.pallas_call(
        _experts_kernel,
        grid_spec=grid_spec,
        out_shape=jax.ShapeDtypeStruct((ROW_SLABS, MOE_ROWS * SUBLANES, LANES), F32),
        compiler_params=_cparams(1, 58),
        name="moe_experts",
    )(*plan, xs, w_gate, w_up, w_down)


COMBINE_TM = 256


def _combine_kernel(pos0_ref, pos1_ref, x_ref, gate_ref, wcol_ref, fg_ref, ys_ref, o_ref, buf_ref, sem,
                    *, final):
    tm = COMBINE_TM
    i = pl.program_id(0)
    n = pl.num_programs(0)
    slot = i % 2

    def issue(step, s):
        def body(r, carry):
            tok = step * tm + r
            dst_rows = pl.ds(pl.multiple_of(r * SUBLANES, SUBLANES), SUBLANES)
            for k, pos_ref in enumerate((pos0_ref, pos1_ref)):
                row = pl.multiple_of(pos_ref[tok] * SUBLANES, SUBLANES)
                _row_copy(ys_ref.at[:, pl.ds(row, SUBLANES), :], buf_ref.at[s, k, :, dst_rows, :],
                          sem.at[s]).start()
            return carry

        lax.fori_loop(0, tm, body, 0)

    @pl.when(i == 0)
    def _():
        issue(0, 0)

    @pl.when(i + 1 < n)
    def _():
        issue(i + 1, 1 - slot)

    for k in range(2):
        _row_copy(ys_ref.at[:, pl.ds(0, tm * SUBLANES), :], buf_ref.at[slot, k], sem.at[slot]).wait()

    y0 = _from_row_tiles(buf_ref.at[slot, 0], tm)
    y1 = _from_row_tiles(buf_ref.at[slot, 1], tm)
    out = x_ref[...] + gate_ref[0] * (wcol_ref[:, 0:1] * y0 + wcol_ref[:, 1:2] * y1)
    if final:
        out = (out * lax.rsqrt(jnp.mean(out * out, axis=-1, keepdims=True) + EPS)) * fg_ref[...]
    o_ref[...] = out


def _combine(pos0, pos1, x2d, gate, wcol, final_g, ys, final):
    tm = COMBINE_TM
    per_batch = SEQ // tm
    grid_spec = pltpu.PrefetchScalarGridSpec(
        num_scalar_prefetch=2,
        grid=(TOKENS // tm,),
        in_specs=[pl.BlockSpec((tm, D_MODEL), lambda i, p0, p1: (i, 0)),
                  pl.BlockSpec((1, 1, D_MODEL), lambda i, p0, p1: (i // per_batch, 0, 0)),
                  pl.BlockSpec((tm, LANES), lambda i, p0, p1: (i, 0)),
                  pl.BlockSpec((1, D_MODEL), lambda i, p0, p1: (0, 0)),
                  pl.BlockSpec(memory_space=pl.ANY)],
        out_specs=pl.BlockSpec((tm, D_MODEL), lambda i, p0, p1: (i, 0)),
        scratch_shapes=[pltpu.VMEM((2, 2, ROW_SLABS, tm * SUBLANES, LANES), F32),
                        pltpu.SemaphoreType.DMA((2,))])
    return pl.pallas_call(
        functools.partial(_combine_kernel, final=final),
        grid_spec=grid_spec,
        out_shape=jax.ShapeDtypeStruct((TOKENS, D_MODEL), F32),
        compiler_params=_cparams(1, 48),
        name="moe_combine",
    )(pos0, pos1, x2d, gate, wcol, final_g, ys)


def _moe_plan(counts):
    counts = counts.astype(I32)
    tiles = (counts + (MOE_TM - 1)) // MOE_TM
    cum = jnp.cumsum(tiles)
    first_tile = cum - tiles
    n_used = cum[-1]
    j = jnp.arange(MOE_TILES, dtype=I32)
    te = jnp.minimum(jnp.sum((j[:, None] >= cum[None, :]).astype(I32), axis=1), N_EXPERTS - 1)
    valid = j < n_used
    te = jnp.where(valid, te, jnp.take(te, n_used - 1))
    tf = (valid & (j == jnp.take(first_tile, te))).astype(I32)
    ti = jnp.where(valid, j, n_used - 1)
    to = jnp.where(valid, j, MOE_TILES)
    return (te, tf, valid.astype(I32), ti, to), first_tile * MOE_TM


def _moe_layer(x2d, layer, mods, norm_g, w_group, b_group, w_expert, b_expert, w_gate, w_up, w_down,
               final_g, final):
    shift, scale, gate = mods
    w_t = jnp.concatenate([w_expert.T, w_group.T,
                           jnp.zeros((ROUTER_ROWS - N_EXPERTS - N_GROUPS, D_MODEL), F32)], axis=0)
    b_rows = jnp.concatenate([b_expert, b_group, jnp.zeros((ROUTER_ROWS - N_EXPERTS - N_GROUPS,), F32)])
    b_rows = jnp.broadcast_to(b_rows[:, None], (ROUTER_ROWS, ROUTER_TM))
    ei, wcol, cnt = _router(x2d, norm_g, scale, shift, w_t, b_rows)
    plan, row_off = _moe_plan(cnt[:, 0])
    pos0 = jnp.take(row_off, ei[0]) + ei[2]
    pos1 = jnp.take(row_off, ei[1]) + ei[3]
    xs = _dispatch(pos0, pos1, x2d, norm_g, scale, shift)
    ys = _experts(plan, xs, w_gate, w_up, w_down, layer)
    return _combine(pos0, pos1, x2d, gate, wcol, final_g, ys, final)


def _split_mods(mod, layer):
    m = mod[layer]
    return tuple(m[:, k * D_MODEL:(k + 1) * D_MODEL].reshape(BATCH, 1, D_MODEL) for k in range(3))


def kernel(x, c, positions, w_mod_mix, b_mod_mix, norm_mix, w_mod_ffn, b_mod_ffn, norm_ffn, ev_w_in, ev_q_norm, ev_w_q_up, ev_kv_norm, ev_w_kv_up, ev_w_out, od_w_in, od_w_out, moe_w_group, moe_b_group, moe_w_expert, moe_b_expert, moe_w_gate, moe_w_up, moe_w_down, final_norm):
    x2d = x.reshape(TOKENS, D_MODEL)
    c_lanes = jnp.broadcast_to(c[:, :, None], (BATCH, D_MODEL, LANES))
    mod_mix = _mods(c_lanes, w_mod_mix, b_mod_mix)
    mod_ffn = _mods(c_lanes, w_mod_ffn, b_mod_ffn)
    pos_lanes = jnp.broadcast_to(positions.reshape(TOKENS, 1), (TOKENS, LANES))
    cos_m, sin_m, cos_r, sin_r = _rope_tables(pos_lanes)
    final_g = final_norm.reshape(1, D_MODEL)

    shift, scale, gate = _split_mods(mod_mix, 0)
    half = MLA_ROPE // 2
    w_in = ev_w_in[0]
    c_kr0 = EVEN_IN_WIDTH_ROPE0 = 3 * SB_WIDTH + MLA_Q_LORA + MLA_KV_LORA
    zc = lambda n: jnp.zeros((D_MODEL, n), F32)
    w_in_pad = jnp.concatenate([w_in[:, :c_kr0], w_in[:, c_kr0:c_kr0 + half], zc(half),
                                w_in[:, c_kr0 + half:], zc(half), zc(LANES)], axis=1)
    proj = _in_proj(x2d, norm_mix[0].reshape(1, D_MODEL), scale, shift, w_in_pad)
    wq = ev_w_q_up[0].reshape(MLA_Q_LORA, MLA_HEADS, MLA_NOPE + MLA_ROPE)
    zq = jnp.zeros((MLA_Q_LORA, MLA_HEADS, half), F32)
    wq = jnp.concatenate([wq[:, :, :MLA_NOPE], wq[:, :, MLA_NOPE:MLA_NOPE + half], zq,
                          wq[:, :, MLA_NOPE + half:], zq], axis=2).reshape(MLA_Q_LORA, MLA_HEADS * MLA_QK_PAD)
    wkv = ev_w_kv_up[0].reshape(MLA_KV_LORA, MLA_HEADS, MLA_NOPE + MLA_V)
    wkv = jnp.concatenate([wkv[:, :, :MLA_NOPE].reshape(MLA_KV_LORA, MLA_HEADS * MLA_NOPE),
                           wkv[:, :, MLA_NOPE:].reshape(MLA_KV_LORA, MLA_HEADS * MLA_V)], axis=1)
    q_cat, k_cat, v_mla = _mla_prep(proj, ev_q_norm[0].reshape(1, MLA_Q_LORA),
                                    ev_kv_norm[0].reshape(1, MLA_KV_LORA), wq, wkv, cos_m, sin_m)
    o_sb = _sb_attention(proj)
    o_mla = _mla_attention(q_cat, k_cat, v_mla)
    x2d = _out_proj([o_sb, o_mla], ev_w_out[0], x2d, gate)
    x2d = _moe_layer(x2d, 0, _split_mods(mod_ffn, 0), norm_ffn[0].reshape(1, D_MODEL),
                     moe_w_group[0], moe_b_group[0], moe_w_expert[0], moe_b_expert[0],
                     moe_w_gate, moe_w_up, moe_w_down, final_g, False)

    shift, scale, gate = _split_mods(mod_mix, 1)
    proj = _in_proj(x2d, norm_mix[1].reshape(1, D_MODEL), scale, shift, od_w_in[0])
    o_ret = _retention(proj, cos_r, sin_r)
    x2d = _out_proj([o_ret], od_w_out[0], x2d, gate)
    x2d = _moe_layer(x2d, 1, _split_mods(mod_ffn, 1), norm_ffn[1].reshape(1, D_MODEL),
                     moe_w_group[1], moe_b_group[1], moe_w_expert[1], moe_b_expert[1],
                     moe_w_gate, moe_w_up, moe_w_down, final_g, True)
    return x2d.reshape(BATCH, SEQ, D_MODEL)
```

```python
import functools
import math

import jax
import jax.numpy as jnp
from jax import lax
from jax.experimental import pallas as pl
from jax.experimental.pallas import tpu as pltpu

F32 = jnp.float32
BF16 = jnp.bfloat16
I32 = jnp.int32

D_MODEL = 2048
BATCH = 4
SEQ = 2048
DEPTH = 2
TOKENS = BATCH * SEQ

HEAD_DIM = 128
SB_HEADS = 8
SB_WIDTH = SB_HEADS * HEAD_DIM
MLA_HEADS = 8
MLA_Q_LORA = 512
MLA_KV_LORA = 256
MLA_NOPE = 128
MLA_ROPE = 64
MLA_V = 128
MLA_QK_PAD = 256
EVEN_IN_PAD = 4096
RET_HEADS = 8
RET_QK = 256
RET_V = 512
RET_QK_WIDTH = RET_HEADS * RET_QK
RET_V_WIDTH = RET_HEADS * RET_V
ODD_IN_WIDTH = 2 * RET_QK_WIDTH + 2 * RET_V_WIDTH
N_GROUPS = 4
EXPERTS_PER_GROUP = 8
N_EXPERTS = N_GROUPS * EXPERTS_PER_GROUP
EXPERT_HIDDEN = 512
CHUNK = 128
ROPE_BASE = 10000.0
EPS = 1e-6

LANES = 128
SUBLANES = 8
ROW_SLABS = D_MODEL // (SUBLANES * LANES)
ROW_CHUNKS = D_MODEL // LANES

ROUTER_ROWS = 48
MOE_TM = 256
MOE_TILES = (2 * TOKENS) // MOE_TM + N_EXPERTS
MOE_ROWS = MOE_TILES * MOE_TM
ZERO_ROWS = MOE_TM // 2

NT_DIMS = (((1,), (1,)), ((), ()))


def _cparams(n_grid, vmem_mb):
    return pltpu.CompilerParams(dimension_semantics=("arbitrary",) * n_grid,
                                vmem_limit_bytes=vmem_mb * 1024 * 1024)


def _norm_mod(x, g, scale, shift):
    y = x * lax.rsqrt(jnp.mean(x * x, axis=-1, keepdims=True) + EPS)
    return (y * g) * (1.0 + scale) + shift


def _to_row_tiles(ref_at, val, rows):
    for c in range(ROW_CHUNKS):
        a, b = divmod(c, SUBLANES)
        ref_at[a, pl.ds(b, rows, stride=SUBLANES), :] = val[:, c * LANES:(c + 1) * LANES]


def _from_row_tiles(ref_at, rows):
    parts = []
    for c in range(ROW_CHUNKS):
        a, b = divmod(c, SUBLANES)
        parts.append(ref_at[a, pl.ds(b, rows, stride=SUBLANES), :])
    return jnp.concatenate(parts, axis=1)


def _mods_kernel(c_ref, w_ref, b_ref, o_ref, cond_ref):
    @pl.when((pl.program_id(0) == 0) & (pl.program_id(1) == 0))
    def _():
        c = c_ref[...]
        cond_ref[...] = c * jax.nn.sigmoid(c)

    tn = w_ref.shape[2]
    rows = []
    for b in range(BATCH):
        cb = cond_ref[b]
        cols = [jnp.sum(w_ref[0, :, j * LANES:(j + 1) * LANES] * cb, axis=0, keepdims=True)
                for j in range(tn // LANES)]
        rows.append(jnp.concatenate(cols, axis=1))
    o_ref[0] = jnp.concatenate(rows, axis=0) + b_ref[0]


def _mods(c_lanes, w_mod, b_mod):
    tn = 512
    n3 = 3 * D_MODEL
    return pl.pallas_call(
        _mods_kernel,
        grid=(DEPTH, n3 // tn),
        in_specs=[pl.BlockSpec((BATCH, D_MODEL, LANES), lambda l, j: (0, 0, 0)),
                  pl.BlockSpec((1, D_MODEL, tn), lambda l, j: (l, 0, j)),
                  pl.BlockSpec((1, 1, tn), lambda l, j: (l, 0, j))],
        out_specs=pl.BlockSpec((1, BATCH, tn), lambda l, j: (l, 0, j)),
        out_shape=jax.ShapeDtypeStruct((DEPTH, BATCH, n3), F32),
        scratch_shapes=[pltpu.VMEM((BATCH, D_MODEL, LANES), F32)],
        compiler_params=_cparams(2, 32),
        name="adaln_mods",
    )(c_lanes, w_mod, b_mod.reshape(DEPTH, 1, n3))


def _rope_tables_kernel(pos_ref, fm_ref, sg_ref, fr_ref, cm_ref, sm_ref, cr_ref, sr_ref):
    p = pos_ref[...].astype(F32)
    am = p * fm_ref[...]
    cm_ref[...] = jnp.cos(am)
    sm_ref[...] = jnp.sin(am) * sg_ref[...]
    ar = p * fr_ref[...]
    cr_ref[...] = jnp.cos(ar)
    sr_ref[...] = jnp.sin(ar)


def _rope_tables(pos_lanes):
    half = MLA_ROPE // 2
    f_mla = jnp.exp(-math.log(ROPE_BASE) * jnp.arange(half, dtype=F32) / half)
    z = jnp.zeros((half,), F32)
    fm = jnp.concatenate([f_mla, z, f_mla, z]).reshape(1, LANES)
    sg = jnp.concatenate([-jnp.ones((half,), F32), z, jnp.ones((half,), F32), z]).reshape(1, LANES)
    hr = RET_QK // 2
    fr = jnp.exp(-math.log(ROPE_BASE) * jnp.arange(hr, dtype=F32) / hr).reshape(1, LANES)
    tm = 1024
    row = pl.BlockSpec((tm, LANES), lambda i: (i, 0))
    vec = pl.BlockSpec((1, LANES), lambda i: (0, 0))
    tab = jax.ShapeDtypeStruct((TOKENS, LANES), F32)
    return pl.pallas_call(
        _rope_tables_kernel,
        grid=(TOKENS // tm,),
        in_specs=[row, vec, vec, vec],
        out_specs=[row, row, row, row],
        out_shape=[tab, tab, tab, tab],
        compiler_params=_cparams(1, 32),
        name="rope_tables",
    )(pos_lanes, fm, sg, fr)


def _in_proj_kernel(x_ref, g_ref, sc_ref, sh_ref, w_ref, o_ref, h_ref):
    @pl.when(pl.program_id(1) == 0)
    def _():
        h_ref[...] = _norm_mod(x_ref[...], g_ref[...], sc_ref[0], sh_ref[0]).astype(BF16)

    o_ref[...] = jnp.dot(h_ref[...], w_ref[...].astype(BF16),
                         preferred_element_type=F32).astype(o_ref.dtype)


def _in_proj(x2d, g, scale, shift, w):
    tm, tn = 1024, 512
    n = w.shape[1]
    per_batch = SEQ // tm
    mod = pl.BlockSpec((1, 1, D_MODEL), lambda i, j: (i // per_batch, 0, 0))
    return pl.pallas_call(
        _in_proj_kernel,
        grid=(TOKENS // tm, n // tn),
        in_specs=[pl.BlockSpec((tm, D_MODEL), lambda i, j: (i, 0)),
                  pl.BlockSpec((1, D_MODEL), lambda i, j: (0, 0)),
                  mod, mod,
                  pl.BlockSpec((D_MODEL, tn), lambda i, j: (0, j))],
        out_specs=pl.BlockSpec((tm, tn), lambda i, j: (i, j)),
        out_shape=jax.ShapeDtypeStruct((TOKENS, n), BF16),
        scratch_shapes=[pltpu.VMEM((tm, D_MODEL), BF16)],
        compiler_params=_cparams(2, 48),
        name="in_proj",
    )(x2d, g, scale, shift, w)


def _mla_prep_kernel(cq_ref, ckv_ref, kr_ref, qn_ref, kvn_ref, wq_ref, wkv_ref, cos_ref, sin_ref,
                     q_ref, k_ref, v_ref):
    cos = cos_ref[...]
    sin = sin_ref[...]

    def rope(x):
        return x * cos + pltpu.roll(x, LANES // 2, 1) * sin

    def rms(x, g):
        xf = x.astype(F32)
        return (xf * lax.rsqrt(jnp.mean(xf * xf, axis=-1, keepdims=True) + EPS)) * g

    cq = rms(cq_ref[...], qn_ref[...]).astype(BF16)
    q = jnp.dot(cq, wq_ref[...].astype(BF16), preferred_element_type=F32)
    ckv = rms(ckv_ref[...], kvn_ref[...]).astype(BF16)
    kv = jnp.dot(ckv, wkv_ref[...].astype(BF16), preferred_element_type=F32)
    kr = rope(kr_ref[...].astype(F32)).astype(BF16)
    scale = (MLA_NOPE + MLA_ROPE) ** -0.5
    for h in range(MLA_HEADS):
        c0 = h * MLA_QK_PAD
        q_ref[:, c0:c0 + MLA_NOPE] = (q[:, c0:c0 + MLA_NOPE] * scale).astype(BF16)
        q_ref[:, c0 + MLA_NOPE:c0 + MLA_QK_PAD] = (
            rope(q[:, c0 + MLA_NOPE:c0 + MLA_QK_PAD]) * scale).astype(BF16)
        k_ref[:, c0:c0 + MLA_NOPE] = kv[:, h * MLA_NOPE:(h + 1) * MLA_NOPE].astype(BF16)
        k_ref[:, c0 + MLA_NOPE:c0 + MLA_QK_PAD] = kr
    v_ref[...] = kv[:, MLA_HEADS * MLA_NOPE:].astype(BF16)


def _mla_prep(proj, q_norm, kv_norm, wq, wkv, cos_m, sin_m):
    tm = 512
    c_q0 = 3 * SB_WIDTH
    c_kv0 = c_q0 + MLA_Q_LORA
    c_kr0 = c_kv0 + MLA_KV_LORA
    full = lambda shape: pl.BlockSpec(shape, lambda i: (0, 0))
    row = lambda w: pl.BlockSpec((tm, w), lambda i: (i, 0))
    qk_w = MLA_HEADS * MLA_QK_PAD
    return pl.pallas_call(
        _mla_prep_kernel,
        grid=(TOKENS // tm,),
        in_specs=[pl.BlockSpec((tm, MLA_Q_LORA), lambda i: (i, c_q0 // MLA_Q_LORA)),
                  pl.BlockSpec((tm, MLA_KV_LORA), lambda i: (i, c_kv0 // MLA_KV_LORA)),
                  pl.BlockSpec((tm, LANES), lambda i: (i, c_kr0 // LANES)),
                  full((1, MLA_Q_LORA)), full((1, MLA_KV_LORA)),
                  full((MLA_Q_LORA, qk_w)), full((MLA_KV_LORA, 2 * MLA_HEADS * MLA_NOPE)),
                  row(LANES), row(LANES)],
        out_specs=[row(qk_w), row(qk_w), row(MLA_HEADS * MLA_V)],
        out_shape=[jax.ShapeDtypeStruct((TOKENS, qk_w), BF16),
                   jax.ShapeDtypeStruct((TOKENS, qk_w), BF16),
                   jax.ShapeDtypeStruct((TOKENS, MLA_HEADS * MLA_V), BF16)],
        compiler_params=_cparams(1, 48),
        name="mla_prep",
    )(proj, proj, proj, q_norm, kv_norm, wq, wkv, cos_m, sin_m)


ATT_T = 256
ATT_HEADS = 2
LOG2E = math.log2(math.e)
SB_SKIP_LOG2 = -160.0


def _sb_attn_kernel(q_ref, k_ref, v_ref, o_ref, acc_ref, run_ref):
    t = ATT_T
    ii = lax.broadcasted_iota(I32, (t, t), 0)
    jj = lax.broadcasted_iota(I32, (t, t), 1)
    strict = jj < ii
    upper2 = (lax.broadcasted_iota(I32, (2 * t, t), 0) % t > lax.broadcasted_iota(I32, (2 * t, t), 1)
              ).astype(BF16)
    z_scale = HEAD_DIM ** -0.5 * LOG2E

    def block(q, kb, cols, run, diag):
        rows = pl.ds(pl.multiple_of(kb * t, t), t)
        z = lax.dot_general(q, k_ref[rows, cols], NT_DIMS, preferred_element_type=F32) * z_scale
        log_beta = jnp.minimum(z, 0.0) - jnp.log2(1.0 + jnp.exp2(jnp.minimum(z, -z)))
        log_keep = log_beta - z
        if diag:
            log_keep = jnp.where(strict, log_keep, 0.0)
        hi = log_keep.astype(BF16)
        lo = (log_keep - hi.astype(F32)).astype(BF16)
        within = jnp.dot(jnp.concatenate([hi, lo], axis=1), upper2, preferred_element_type=F32)
        a = jnp.exp2(log_beta + (within + run))
        if diag:
            a = jnp.where(strict, a, 0.0)
        pv = jnp.dot(a.astype(BF16), v_ref[rows, cols], preferred_element_type=F32)
        return pv, run + (within[:, 0:1] + log_keep[:, 0:1])

    nq = SEQ // t
    for qi in range(nq):
        qrows = pl.ds(qi * t, t)
        for hh in range(ATT_HEADS):
            cols = slice(hh * HEAD_DIM, (hh + 1) * HEAD_DIM)
            q = q_ref[qrows, cols]
            acc, run = block(q, qi, cols, jnp.zeros((t, 1), F32), True)
            if qi > 0:
                pv, run = block(q, qi - 1, cols, run, False)
                acc = acc + pv
            acc_ref[qrows, cols] = acc
            run_ref[qrows, cols] = jnp.broadcast_to(run, (t, HEAD_DIM))

    def more(qrows):
        return (jnp.max(run_ref[qrows, :]) > SB_SKIP_LOG2).astype(I32)

    for qi in range(2, nq):
        qrows = pl.ds(qi * t, t)

        def body(c, qrows=qrows):
            kb = c[0]
            for hh in range(ATT_HEADS):
                cols = slice(hh * HEAD_DIM, (hh + 1) * HEAD_DIM)
                pv, run = block(q_ref[qrows, cols], kb, cols, run_ref[qrows, hh * HEAD_DIM:hh * HEAD_DIM + 1],
                                False)
                acc_ref[qrows, cols] = acc_ref[qrows, cols] + pv
                run_ref[qrows, cols] = jnp.broadcast_to(run, (t, HEAD_DIM))
            return kb - 1, more(qrows)

        lax.while_loop(lambda c: (c[0] >= 0) & (c[1] == 1), body, (jnp.int32(qi - 2), more(qrows)))

    o_ref[...] = acc_ref[...].astype(o_ref.dtype)


def _sb_attention(proj):
    w = ATT_HEADS * HEAD_DIM
    blk = lambda off: pl.BlockSpec((SEQ, w), lambda b, h: (b, off + h))
    n = SB_HEADS // ATT_HEADS
    return pl.pallas_call(
        _sb_attn_kernel,
        grid=(BATCH, n),
        in_specs=[blk(0), blk(n), blk(2 * n)],
        out_specs=pl.BlockSpec((SEQ, w), lambda b, h: (b, h)),
        out_shape=jax.ShapeDtypeStruct((TOKENS, SB_WIDTH), BF16),
        scratch_shapes=[pltpu.VMEM((SEQ, w), F32), pltpu.VMEM((SEQ, w), F32)],
        compiler_params=_cparams(2, 48),
        name="sb_attention",
    )(proj, proj, proj)


def _mla_attn_kernel(q_ref, k_ref, v_ref, o_ref):
    t = ATT_T
    ii = lax.broadcasted_iota(I32, (t, t), 0)
    jj = lax.broadcasted_iota(I32, (t, t), 1)
    causal = jj <= ii
    neg = -1e30

    def step(q, kb, hh, carry, diag):
        m, l, acc = carry
        rows = pl.ds(pl.multiple_of(kb * t, t), t)
        s = lax.dot_general(q, k_ref[rows, hh * MLA_QK_PAD:(hh + 1) * MLA_QK_PAD], NT_DIMS,
                            preferred_element_type=F32)
        if diag:
            s = jnp.where(causal, s, neg)
        m_new = jnp.maximum(m, jnp.max(s, axis=1, keepdims=True))
        alpha = jnp.exp(m - m_new)
        p = jnp.exp(s - m_new)
        l = alpha * l + jnp.sum(p, axis=1, keepdims=True)
        acc = alpha * acc + jnp.dot(p.astype(BF16), v_ref[rows, hh * MLA_V:(hh + 1) * MLA_V],
                                    preferred_element_type=F32)
        return m_new, l, acc

    for qi in range(SEQ // t):
        qrows = pl.ds(qi * t, t)
        for hh in range(ATT_HEADS):
            q = q_ref[qrows, hh * MLA_QK_PAD:(hh + 1) * MLA_QK_PAD]
            c = (jnp.full((t, 1), neg, F32), jnp.zeros((t, 1), F32), jnp.zeros((t, MLA_V), F32))
            for kb in range(qi):
                c = step(q, kb, hh, c, False)
            _, l, acc = step(q, qi, hh, c, True)
            o_ref[qrows, hh * MLA_V:(hh + 1) * MLA_V] = (acc / l).astype(o_ref.dtype)


def _mla_attention(q_cat, k_cat, v):
    qk = pl.BlockSpec((SEQ, ATT_HEADS * MLA_QK_PAD), lambda b, h: (b, h))
    vo = pl.BlockSpec((SEQ, ATT_HEADS * MLA_V), lambda b, h: (b, h))
    return pl.pallas_call(
        _mla_attn_kernel,
        grid=(BATCH, MLA_HEADS // ATT_HEADS),
        in_specs=[qk, qk, vo],
        out_specs=vo,
        out_shape=jax.ShapeDtypeStruct((TOKENS, MLA_HEADS * MLA_V), BF16),
        compiler_params=_cparams(2, 48),
        name="mla_attention",
    )(q_cat, k_cat, v)


def _out_proj_kernel(*refs, n_in):
    y_refs = refs[:n_in]
    w_ref, x_ref, gate_ref, o_ref = refs[n_in:]
    acc = None
    off = 0
    for y_ref in y_refs:
        kk = y_ref.shape[1]
        d = jnp.dot(y_ref[...], w_ref[off:off + kk, :].astype(BF16), preferred_element_type=F32)
        acc = d if acc is None else acc + d
        off += kk
    o_ref[...] = x_ref[...] + gate_ref[0] * acc


def _out_proj(ys, w, x2d, gate):
    tm, tn = 1024, 512
    per_batch = SEQ // tm
    in_specs = [pl.BlockSpec((tm, y.shape[1]), lambda i, j: (i, 0)) for y in ys]
    in_specs += [pl.BlockSpec((w.shape[0], tn), lambda i, j: (0, j)),
                 pl.BlockSpec((tm, tn), lambda i, j: (i, j)),
                 pl.BlockSpec((1, 1, tn), lambda i, j: (i // per_batch, 0, j))]
    return pl.pallas_call(
        functools.partial(_out_proj_kernel, n_in=len(ys)),
        grid=(TOKENS // tm, D_MODEL // tn),
        in_specs=in_specs,
        out_specs=pl.BlockSpec((tm, tn), lambda i, j: (i, j)),
        out_shape=jax.ShapeDtypeStruct((TOKENS, D_MODEL), F32),
        compiler_params=_cparams(2, 56),
        name="out_proj",
    )(*ys, w, x2d, gate)


def _retention_kernel(lg_ref, q_ref, k_ref, v_ref, g_ref, cos_ref, sin_ref, o_ref, state_ref):
    lg = lg_ref[pl.program_id(1)]
    c = CHUNK
    half = RET_QK // 2
    ii = lax.broadcasted_iota(I32, (c, c), 0)
    jj = lax.broadcasted_iota(I32, (c, c), 1)
    rel = (ii - jj).astype(F32)
    intra = jnp.where(rel >= 0, jnp.exp(jnp.maximum(rel, 0.0) * lg), 0.0)
    idx = lax.broadcasted_iota(I32, (c, 1), 0).astype(F32)
    q_decay = jnp.exp((idx + 1.0) * lg)
    k_decay = jnp.exp((c - 1.0 - idx) * lg)
    chunk_decay = jnp.exp(jnp.full((1, 1), float(c), F32) * lg)
    state_ref[...] = jnp.zeros_like(state_ref)

    def rope(x, cos, sin):
        x1, x2 = x[:, :half], x[:, half:]
        return jnp.concatenate([x1 * cos - x2 * sin, x1 * sin + x2 * cos], axis=1)

    for ci in range(SEQ // c):
        rows = pl.ds(ci * c, c)
        cos = cos_ref[rows, :]
        sin = sin_ref[rows, :]
        qr = rope(q_ref[rows, :].astype(F32), cos, sin)
        kr = rope(k_ref[rows, :].astype(F32), cos, sin) * (RET_QK ** -0.5)
        v = v_ref[rows, :]
        qb = qr.astype(BF16)
        scores = lax.dot_general(qb, kr.astype(BF16), NT_DIMS, preferred_element_type=F32) * intra
        inner = jnp.dot(scores.astype(BF16), v, preferred_element_type=F32)
        st = state_ref[...]
        cross = jnp.dot(qb, st.astype(BF16), preferred_element_type=F32) * q_decay
        kd_t = jnp.transpose(kr * k_decay).astype(BF16)
        state_ref[...] = st * chunk_decay + jnp.dot(kd_t, v, preferred_element_type=F32)
        o = inner + cross
        o = o * lax.rsqrt(jnp.mean(o * o, axis=-1, keepdims=True) + EPS)
        gg = g_ref[rows, :].astype(F32)
        o_ref[rows, :] = ((gg * jax.nn.sigmoid(gg)) * o).astype(o_ref.dtype)


def _retention(proj, cos_r, sin_r):
    log_gamma = jnp.log1p(-jnp.exp2(-5.0 - jnp.arange(RET_HEADS, dtype=F32)))
    k0 = RET_QK_WIDTH // RET_QK
    v0 = 2 * RET_QK_WIDTH // RET_V
    g0 = (2 * RET_QK_WIDTH + RET_V_WIDTH) // RET_V
    tab = pl.BlockSpec((SEQ, LANES), lambda b, h: (b, 0))
    return pl.pallas_call(
        _retention_kernel,
        grid=(BATCH, RET_HEADS),
        in_specs=[pl.BlockSpec(memory_space=pltpu.SMEM),
                  pl.BlockSpec((SEQ, RET_QK), lambda b, h: (b, h)),
                  pl.BlockSpec((SEQ, RET_QK), lambda b, h: (b, k0 + h)),
                  pl.BlockSpec((SEQ, RET_V), lambda b, h: (b, v0 + h)),
                  pl.BlockSpec((SEQ, RET_V), lambda b, h: (b, g0 + h)),
                  tab, tab],
        out_specs=pl.BlockSpec((SEQ, RET_V), lambda b, h: (b, h)),
        out_shape=jax.ShapeDtypeStruct((TOKENS, RET_V_WIDTH), BF16),
        scratch_shapes=[pltpu.VMEM((RET_QK, RET_V), F32)],
        compiler_params=_cparams(2, 48),
        name="retention",
    )(log_gamma, proj, proj, proj, proj, cos_r, sin_r)


ROUTER_TM = 512


def _router_kernel(x_ref, g_ref, sc_ref, sh_ref, w_ref, b_ref, ei_ref, wcol_ref, cnt_ref, carry_ref):
    tm = ROUTER_TM

    @pl.when(pl.program_id(0) == 0)
    def _():
        carry_ref[...] = jnp.zeros_like(carry_ref)

    def split(a):
        hi = a.astype(BF16)
        return hi, (a - hi.astype(F32)).astype(BF16)

    h = _norm_mod(x_ref[...], g_ref[...], sc_ref[0], sh_ref[0])
    h_hi, h_lo = split(h)
    w_hi, w_lo = split(w_ref[...])
    nt = lambda a, b: lax.dot_general(a, b, NT_DIMS, preferred_element_type=F32)
    logits = (nt(w_hi, h_hi) + nt(w_hi, h_lo) + nt(w_lo, h_hi)) + b_ref[...]
    e_log = logits[0:N_EXPERTS]
    g_log = logits[N_EXPERTS:N_EXPERTS + N_GROUPS]

    def top1(vals, n):
        rows = lax.broadcasted_iota(I32, (n, tm), 0)
        m = jnp.max(vals, axis=0, keepdims=True)
        return m, jnp.min(jnp.where(vals == m, rows, n), axis=0, keepdims=True), rows

    g_max, g_idx, _ = top1(g_log, N_GROUPS)
    g_w = 1.0 / jnp.sum(jnp.exp(g_log - g_max), axis=0, keepdims=True)
    sel = e_log[0:EXPERTS_PER_GROUP]
    for gi in range(1, N_GROUPS):
        sel = jnp.where(g_idx == gi, e_log[gi * EXPERTS_PER_GROUP:(gi + 1) * EXPERTS_PER_GROUP], sel)
    m1, i1, rows8 = top1(sel, EXPERTS_PER_GROUP)
    m2, i2, _ = top1(jnp.where(rows8 == i1, -jnp.inf, sel), EXPERTS_PER_GROUP)
    ratio = jnp.exp(m2 - m1)
    w1 = g_w / (1.0 + ratio)
    w2 = (g_w * ratio) / (1.0 + ratio)
    e1 = g_idx * EXPERTS_PER_GROUP + i1
    e2 = g_idx * EXPERTS_PER_GROUP + i2

    rows32 = lax.broadcasted_iota(I32, (N_EXPERTS, tm), 0)
    hit1 = rows32 == e1
    hit2 = rows32 == e2
    onehot = jnp.where(hit1 | hit2, 1.0, 0.0)
    jj = lax.broadcasted_iota(I32, (tm, tm), 0)
    tt = lax.broadcasted_iota(I32, (tm, tm), 1)
    before = (jj < tt).astype(BF16)
    rank_e = jnp.dot(onehot.astype(BF16), before, preferred_element_type=F32) + carry_ref[:, 0:1]
    r1 = jnp.sum(jnp.where(hit1, rank_e, 0.0), axis=0, keepdims=True)
    r2 = jnp.sum(jnp.where(hit2, rank_e, 0.0), axis=0, keepdims=True)
    carry_ref[...] = carry_ref[...] + jnp.sum(onehot, axis=1, keepdims=True)
    cnt_ref[...] = carry_ref[...]

    ei_ref[...] = jnp.where(rows8 == 0, e1, jnp.where(rows8 == 1, e2, jnp.where(
        rows8 == 2, r1.astype(I32), jnp.where(rows8 == 3, r2.astype(I32), 0))))
    rows128 = lax.broadcasted_iota(I32, (LANES, tm), 0)
    wrows = jnp.where(rows128 == 0, w1, jnp.where(rows128 == 1, w2, 0.0))
    wcol_ref[...] = jnp.transpose(wrows)


def _router(x2d, g, scale, shift, w_t, b_rows):
    tm = ROUTER_TM
    per_batch = SEQ // tm
    mod = pl.BlockSpec((1, 1, D_MODEL), lambda i: (i // per_batch, 0, 0))
    return pl.pallas_call(
        _router_kernel,
        grid=(TOKENS // tm,),
        in_specs=[pl.BlockSpec((tm, D_MODEL), lambda i: (i, 0)),
                  pl.BlockSpec((1, D_MODEL), lambda i: (0, 0)),
                  mod, mod,
                  pl.BlockSpec((ROUTER_ROWS, D_MODEL), lambda i: (0, 0)),
                  pl.BlockSpec((ROUTER_ROWS, tm), lambda i: (0, 0))],
        out_specs=[pl.BlockSpec((SUBLANES, tm), lambda i: (0, i)),
                   pl.BlockSpec((tm, LANES), lambda i: (i, 0)),
                   pl.BlockSpec((N_EXPERTS, LANES), lambda i: (0, 0))],
        out_shape=[jax.ShapeDtypeStruct((SUBLANES, TOKENS), I32),
                   jax.ShapeDtypeStruct((TOKENS, LANES), F32),
                   jax.ShapeDtypeStruct((N_EXPERTS, LANES), F32)],
        scratch_shapes=[pltpu.VMEM((N_EXPERTS, LANES), F32)],
        compiler_params=_cparams(1, 48),
        name="moe_router",
    )(x2d, g, scale, shift, w_t, b_rows)


DISPATCH_TM = 256


def _row_copy(src, dst, sem):
    return pltpu.make_async_copy(src, dst, sem)


def _dispatch_kernel(pos0_ref, pos1_ref, meta_ref, x_ref, g_ref, sc_ref, sh_ref, xs_ref,
                     buf_ref, zero_ref, sem, zsem):
    tm = DISPATCH_TM
    i = pl.program_id(0)
    n = pl.num_programs(0)
    slot = i % 2

    def clear_padding(act):
        def zero_copy(row, nrows):
            return _row_copy(zero_ref.at[:, pl.ds(0, nrows * SUBLANES), :],
                             xs_ref.at[:, pl.ds(pl.multiple_of(row * SUBLANES, SUBLANES), nrows * SUBLANES), :],
                             zsem)

        def per_expert(e, carry):
            cnt = meta_ref[e]
            row = meta_ref[N_EXPERTS + e] + cnt
            pad = (-cnt) & (MOE_TM - 1)
            for bit in (128, 64, 32, 16, 8, 4, 2, 1):
                @pl.when((pad & bit) != 0)
                def _(row=row, bit=bit):
                    act(zero_copy(row, bit))
                row = row + (pad & bit)
            return carry

        lax.fori_loop(0, N_EXPERTS, per_expert, 0)

        def per_tile(tl, carry):
            for part in range(MOE_TM // ZERO_ROWS):
                act(zero_copy(tl * MOE_TM + part * ZERO_ROWS, ZERO_ROWS))
            return carry

        lax.fori_loop(meta_ref[2 * N_EXPERTS], MOE_TILES, per_tile, 0)

    @pl.when(i == 0)
    def _():
        zero_ref[...] = jnp.zeros_like(zero_ref)
        clear_padding(lambda cp: cp.start())

    def wait_slot(s):
        for _ in range(2):
            _row_copy(buf_ref.at[s], xs_ref.at[:, pl.ds(0, tm * SUBLANES), :], sem.at[s]).wait()

    @pl.when(i >= 2)
    def _():
        wait_slot(slot)

    h = _norm_mod(x_ref[...], g_ref[...], sc_ref[0], sh_ref[0])
    _to_row_tiles(buf_ref.at[slot], h, tm)

    def issue(r, carry):
        tok = i * tm + r
        src = buf_ref.at[slot, :, pl.ds(pl.multiple_of(r * SUBLANES, SUBLANES), SUBLANES), :]
        for pos_ref in (pos0_ref, pos1_ref):
            row = pl.multiple_of(pos_ref[tok] * SUBLANES, SUBLANES)
            _row_copy(src, xs_ref.at[:, pl.ds(row, SUBLANES), :], sem.at[slot]).start()
        return carry

    lax.fori_loop(0, tm, issue, 0)

    @pl.when(i == n - 1)
    def _():
        wait_slot(1 - slot)
        wait_slot(slot)
        clear_padding(lambda cp: cp.wait())


def _dispatch(pos0, pos1, meta, x2d, g, scale, shift):
    tm = DISPATCH_TM
    per_batch = SEQ // tm
    mod = pl.BlockSpec((1, 1, D_MODEL), lambda i, *_: (i // per_batch, 0, 0))
    grid_spec = pltpu.PrefetchScalarGridSpec(
        num_scalar_prefetch=3,
        grid=(TOKENS // tm,),
        in_specs=[pl.BlockSpec((tm, D_MODEL), lambda i, *_: (i, 0)),
                  pl.BlockSpec((1, D_MODEL), lambda i, *_: (0, 0)),
                  mod, mod],
        out_specs=pl.BlockSpec(memory_space=pl.ANY),
        scratch_shapes=[pltpu.VMEM((2, ROW_SLABS, tm * SUBLANES, LANES), F32),
                        pltpu.VMEM((ROW_SLABS, ZERO_ROWS * SUBLANES, LANES), F32),
                        pltpu.SemaphoreType.DMA((2,)),
                        pltpu.SemaphoreType.DMA(())])
    return pl.pallas_call(
        _dispatch_kernel,
        grid_spec=grid_spec,
        out_shape=jax.ShapeDtypeStruct((ROW_SLABS, MOE_ROWS * SUBLANES, LANES), F32),
        compiler_params=_cparams(1, 48),
        name="moe_dispatch",
    )(pos0, pos1, meta, x2d, g, scale, shift)


def _experts_kernel(te_ref, tf_ref, tv_ref, ti_ref, xs_ref, wg_ref, wu_ref, wd_ref, ys_ref,
                    wgb_ref, wub_ref, wdb_ref):
    del te_ref, ti_ref
    tm = MOE_TM
    j = pl.program_id(0)

    @pl.when(tf_ref[j] == 1)
    def _():
        wgb_ref[...] = wg_ref[0, 0].astype(BF16)
        wub_ref[...] = wu_ref[0, 0].astype(BF16)
        wdb_ref[...] = wd_ref[0, 0].astype(BF16)

    @pl.when(tv_ref[j] == 1)
    def _():
        x = _from_row_tiles(xs_ref, tm).astype(BF16)
        gate = jnp.dot(x, wgb_ref[...], preferred_element_type=F32)
        up = jnp.dot(x, wub_ref[...], preferred_element_type=F32)
        a = ((gate * jax.nn.sigmoid(gate)) * up).astype(BF16)
        _to_row_tiles(ys_ref, jnp.dot(a, wdb_ref[...], preferred_element_type=F32), tm)

    @pl.when(tv_ref[j] == 0)
    def _():
        ys_ref[...] = jnp.zeros_like(ys_ref)


def _experts(plan, xs, w_gate, w_up, w_down, layer):
    tm = MOE_TM
    rows = pl.BlockSpec((ROW_SLABS, tm * SUBLANES, LANES), lambda j, te, tf, tv, ti: (0, ti[j], 0))
    rows_out = pl.BlockSpec((ROW_SLABS, tm * SUBLANES, LANES), lambda j, te, tf, tv, ti: (0, j, 0))
    w_in = pl.BlockSpec((1, 1, D_MODEL, EXPERT_HIDDEN), lambda j, te, tf, tv, ti: (layer, te[j], 0, 0))
    w_out = pl.BlockSpec((1, 1, EXPERT_HIDDEN, D_MODEL), lambda j, te, tf, tv, ti: (layer, te[j], 0, 0))
    grid_spec = pltpu.PrefetchScalarGridSpec(
        num_scalar_prefetch=4,
        grid=(MOE_TILES,),
        in_specs=[rows, w_in, w_in, w_out],
        out_specs=rows_out,
        scratch_shapes=[pltpu.VMEM((D_MODEL, EXPERT_HIDDEN), BF16),
                        pltpu.VMEM((D_MODEL, EXPERT_HIDDEN), BF16),
                        pltpu.VMEM((EXPERT_HIDDEN, D_MODEL), BF16)])
    return pl.pallas_call(
        _experts_kernel,
        grid_spec=grid_spec,
        out_shape=jax.ShapeDtypeStruct((ROW_SLABS, MOE_ROWS * SUBLANES, LANES), F32),
        compiler_params=_cparams(1, 58),
        name="moe_experts",
    )(*plan, xs, w_gate, w_up, w_down)


COMBINE_TM = 256


def _combine_kernel(pos0_ref, pos1_ref, x_ref, gate_ref, wcol_ref, fg_ref, ys_ref, o_ref, buf_ref, sem,
                    *, final):
    tm = COMBINE_TM
    i = pl.program_id(0)
    n = pl.num_programs(0)
    slot = i % 2

    def issue(step, s):
        def body(r, carry):
            tok = step * tm + r
            dst_rows = pl.ds(pl.multiple_of(r * SUBLANES, SUBLANES), SUBLANES)
            for k, pos_ref in enumerate((pos0_ref, pos1_ref)):
                row = pl.multiple_of(pos_ref[tok] * SUBLANES, SUBLANES)
                _row_copy(ys_ref.at[:, pl.ds(row, SUBLANES), :], buf_ref.at[s, k, :, dst_rows, :],
                          sem.at[s]).start()
            return carry

        lax.fori_loop(0, tm, body, 0)

    @pl.when(i == 0)
    def _():
        issue(0, 0)

    @pl.when(i + 1 < n)
    def _():
        issue(i + 1, 1 - slot)

    for k in range(2):
        _row_copy(ys_ref.at[:, pl.ds(0, tm * SUBLANES), :], buf_ref.at[slot, k], sem.at[slot]).wait()

    y0 = _from_row_tiles(buf_ref.at[slot, 0], tm)
    y1 = _from_row_tiles(buf_ref.at[slot, 1], tm)
    out = x_ref[...] + gate_ref[0] * (wcol_ref[:, 0:1] * y0 + wcol_ref[:, 1:2] * y1)
    if final:
        out = (out * lax.rsqrt(jnp.mean(out * out, axis=-1, keepdims=True) + EPS)) * fg_ref[...]
    o_ref[...] = out


def _combine(pos0, pos1, x2d, gate, wcol, final_g, ys, final):
    tm = COMBINE_TM
    per_batch = SEQ // tm
    grid_spec = pltpu.PrefetchScalarGridSpec(
        num_scalar_prefetch=2,
        grid=(TOKENS // tm,),
        in_specs=[pl.BlockSpec((tm, D_MODEL), lambda i, p0, p1: (i, 0)),
                  pl.BlockSpec((1, 1, D_MODEL), lambda i, p0, p1: (i // per_batch, 0, 0)),
                  pl.BlockSpec((tm, LANES), lambda i, p0, p1: (i, 0)),
                  pl.BlockSpec((1, D_MODEL), lambda i, p0, p1: (0, 0)),
                  pl.BlockSpec(memory_space=pl.ANY)],
        out_specs=pl.BlockSpec((tm, D_MODEL), lambda i, p0, p1: (i, 0)),
        scratch_shapes=[pltpu.VMEM((2, 2, ROW_SLABS, tm * SUBLANES, LANES), F32),
                        pltpu.SemaphoreType.DMA((2,))])
    return pl.pallas_call(
        functools.partial(_combine_kernel, final=final),
        grid_spec=grid_spec,
        out_shape=jax.ShapeDtypeStruct((TOKENS, D_MODEL), F32),
        compiler_params=_cparams(1, 48),
        name="moe_combine",
    )(pos0, pos1, x2d, gate, wcol, final_g, ys)


def _moe_plan(counts):
    counts = counts.astype(I32)
    tiles = (counts + (MOE_TM - 1)) // MOE_TM
    cum = jnp.cumsum(tiles)
    first_tile = cum - tiles
    n_used = cum[-1]
    j = jnp.arange(MOE_TILES, dtype=I32)
    te = jnp.minimum(jnp.sum((j[:, None] >= cum[None, :]).astype(I32), axis=1), N_EXPERTS - 1)
    valid = j < n_used
    te = jnp.where(valid, te, jnp.take(te, n_used - 1))
    tf = (valid & (j == jnp.take(first_tile, te))).astype(I32)
    ti = jnp.where(valid, j, n_used - 1)
    row_off = first_tile * MOE_TM
    meta = jnp.concatenate([counts, row_off, n_used.reshape(1)])
    return (te, tf, valid.astype(I32), ti), row_off, meta


def _positions_kernel(ei_ref, off_ref, pos_ref):
    tm = ei_ref.shape[1]
    rows32 = lax.broadcasted_iota(I32, (N_EXPERTS, tm), 0)
    rows8 = lax.broadcasted_iota(I32, (SUBLANES, tm), 0)
    off = off_ref[:, 0:1]

    def pos(k):
        first = jnp.sum(jnp.where(rows32 == ei_ref[k:k + 1, :], off, 0), axis=0, keepdims=True)
        return first + ei_ref[k + 2:k + 3, :]

    pos_ref[...] = jnp.where(rows8 == 0, pos(0), jnp.where(rows8 == 1, pos(1), 0))


def _positions(ei, row_off):
    tm = 2048
    blk = pl.BlockSpec((SUBLANES, tm), lambda i: (0, i))
    return pl.pallas_call(
        _positions_kernel,
        grid=(TOKENS // tm,),
        in_specs=[blk, pl.BlockSpec((N_EXPERTS, LANES), lambda i: (0, 0))],
        out_specs=blk,
        out_shape=jax.ShapeDtypeStruct((SUBLANES, TOKENS), I32),
        compiler_params=_cparams(1, 32),
        name="moe_positions",
    )(ei, jnp.broadcast_to(row_off[:, None], (N_EXPERTS, LANES)))


def _moe_layer(x2d, layer, mods, norm_g, w_group, b_group, w_expert, b_expert, w_gate, w_up, w_down,
               final_g, final):
    shift, scale, gate = mods
    w_t = jnp.concatenate([w_expert.T, w_group.T,
                           jnp.zeros((ROUTER_ROWS - N_EXPERTS - N_GROUPS, D_MODEL), F32)], axis=0)
    b_rows = jnp.concatenate([b_expert, b_group, jnp.zeros((ROUTER_ROWS - N_EXPERTS - N_GROUPS,), F32)])
    b_rows = jnp.broadcast_to(b_rows[:, None], (ROUTER_ROWS, ROUTER_TM))
    ei, wcol, cnt = _router(x2d, norm_g, scale, shift, w_t, b_rows)
    plan, row_off, meta = _moe_plan(cnt[:, 0])
    pos = _positions(ei, row_off)
    pos0, pos1 = pos[0], pos[1]
    xs = _dispatch(pos0, pos1, meta, x2d, norm_g, scale, shift)
    ys = _experts(plan, xs, w_gate, w_up, w_down, layer)
    return _combine(pos0, pos1, x2d, gate, wcol, final_g, ys, final)


def _split_mods(mod, layer):
    m = mod[layer]
    return tuple(m[:, k * D_MODEL:(k + 1) * D_MODEL].reshape(BATCH, 1, D_MODEL) for k in range(3))


def kernel(x, c, positions, w_mod_mix, b_mod_mix, norm_mix, w_mod_ffn, b_mod_ffn, norm_ffn, ev_w_in, ev_q_norm, ev_w_q_up, ev_kv_norm, ev_w_kv_up, ev_w_out, od_w_in, od_w_out, moe_w_group, moe_b_group, moe_w_expert, moe_b_expert, moe_w_gate, moe_w_up, moe_w_down, final_norm):
    x2d = x.reshape(TOKENS, D_MODEL)
    c_lanes = jnp.broadcast_to(c[:, :, None], (BATCH, D_MODEL, LANES))
    mod_mix = _mods(c_lanes, w_mod_mix, b_mod_mix)
    mod_ffn = _mods(c_lanes, w_mod_ffn, b_mod_ffn)
    pos_lanes = jnp.broadcast_to(positions.reshape(TOKENS, 1), (TOKENS, LANES))
    cos_m, sin_m, cos_r, sin_r = _rope_tables(pos_lanes)
    final_g = final_norm.reshape(1, D_MODEL)

    shift, scale, gate = _split_mods(mod_mix, 0)
    half = MLA_ROPE // 2
    w_in = ev_w_in[0]
    c_kr0 = 3 * SB_WIDTH + MLA_Q_LORA + MLA_KV_LORA
    zc = lambda n: jnp.zeros((D_MODEL, n), F32)
    w_in_pad = jnp.concatenate([w_in[:, :c_kr0], w_in[:, c_kr0:c_kr0 + half], zc(half),
                                w_in[:, c_kr0 + half:], zc(half), zc(LANES)], axis=1)
    proj = _in_proj(x2d, norm_mix[0].reshape(1, D_MODEL), scale, shift, w_in_pad)
    wq = ev_w_q_up[0].reshape(MLA_Q_LORA, MLA_HEADS, MLA_NOPE + MLA_ROPE)
    zq = jnp.zeros((MLA_Q_LORA, MLA_HEADS, half), F32)
    wq = jnp.concatenate([wq[:, :, :MLA_NOPE], wq[:, :, MLA_NOPE:MLA_NOPE + half], zq,
                          wq[:, :, MLA_NOPE + half:], zq], axis=2).reshape(MLA_Q_LORA, MLA_HEADS * MLA_QK_PAD)
    wkv = ev_w_kv_up[0].reshape(MLA_KV_LORA, MLA_HEADS, MLA_NOPE + MLA_V)
    wkv = jnp.concatenate([wkv[:, :, :MLA_NOPE].reshape(MLA_KV_LORA, MLA_HEADS * MLA_NOPE),
                           wkv[:, :, MLA_NOPE:].reshape(MLA_KV_LORA, MLA_HEADS * MLA_V)], axis=1)
    q_cat, k_cat, v_mla = _mla_prep(proj, ev_q_norm[0].reshape(1, MLA_Q_LORA),
                                    ev_kv_norm[0].reshape(1, MLA_KV_LORA), wq, wkv, cos_m, sin_m)
    o_sb = _sb_attention(proj)
    o_mla = _mla_attention(q_cat, k_cat, v_mla)
    x2d = _out_proj([o_sb, o_mla], ev_w_out[0], x2d, gate)
    x2d = _moe_layer(x2d, 0, _split_mods(mod_ffn, 0), norm_ffn[0].reshape(1, D_MODEL),
                     moe_w_group[0], moe_b_group[0], moe_w_expert[0], moe_b_expert[0],
                     moe_w_gate, moe_w_up, moe_w_down, final_g, False)

    shift, scale, gate = _split_mods(mod_mix, 1)
    proj = _in_proj(x2d, norm_mix[1].reshape(1, D_MODEL), scale, shift, od_w_in[0])
    o_ret = _retention(proj, cos_r, sin_r)
    x2d = _out_proj([o_ret], od_w_out[0], x2d, gate)
    x2d = _moe_layer(x2d, 1, _split_mods(mod_ffn, 1), norm_ffn[1].reshape(1, D_MODEL),
                     moe_w_group[1], moe_b_group[1], moe_w_expert[1], moe_b_expert[1],
                     moe_w_gate, moe_w_up, moe_w_down, final_g, True)
    return x2d.reshape(BATCH, SEQ, D_MODEL)
```

```python
import functools
import math

import jax
import jax.numpy as jnp
from jax import lax
from jax.experimental import pallas as pl
from jax.experimental.pallas import tpu as pltpu

F32 = jnp.float32
BF16 = jnp.bfloat16
I32 = jnp.int32

D_MODEL = 2048
BATCH = 4
SEQ = 2048
DEPTH = 2
TOKENS = BATCH * SEQ

HEAD_DIM = 128
SB_HEADS = 8
SB_WIDTH = SB_HEADS * HEAD_DIM
MLA_HEADS = 8
MLA_Q_LORA = 512
MLA_KV_LORA = 256
MLA_NOPE = 128
MLA_ROPE = 64
MLA_V = 128
MLA_QK_PAD = 256
EVEN_IN_PAD = 4096
RET_HEADS = 8
RET_QK = 256
RET_V = 512
RET_QK_WIDTH = RET_HEADS * RET_QK
RET_V_WIDTH = RET_HEADS * RET_V
ODD_IN_WIDTH = 2 * RET_QK_WIDTH + 2 * RET_V_WIDTH
N_GROUPS = 4
EXPERTS_PER_GROUP = 8
N_EXPERTS = N_GROUPS * EXPERTS_PER_GROUP
EXPERT_HIDDEN = 512
CHUNK = 128
ROPE_BASE = 10000.0
EPS = 1e-6

LANES = 128
SUBLANES = 8
HALF_D = D_MODEL // 2
assert HALF_D == SUBLANES * LANES
U32 = jnp.uint32

ROUTER_ROWS = 48
MOE_TM = 256
MOE_TILES = (2 * TOKENS) // MOE_TM + N_EXPERTS
MOE_ROWS = MOE_TILES * MOE_TM
ZERO_ROWS = MOE_TM // 2

NT_DIMS = (((1,), (1,)), ((), ()))


def _cparams(n_grid, vmem_mb):
    return pltpu.CompilerParams(dimension_semantics=("arbitrary",) * n_grid,
                                vmem_limit_bytes=vmem_mb * 1024 * 1024)


def _norm_mod(x, g, scale, shift):
    y = x * lax.rsqrt(jnp.mean(x * x, axis=-1, keepdims=True) + EPS)
    return (y * g) * (1.0 + scale) + shift


def _to_row_tiles(ref_at, val, rows):
    for b in range(SUBLANES):
        lo = val[:, b * LANES:(b + 1) * LANES]
        hi = val[:, HALF_D + b * LANES:HALF_D + (b + 1) * LANES]
        ref_at[pl.ds(b, rows, stride=SUBLANES), :] = pltpu.pack_elementwise([lo, hi], packed_dtype=BF16)


def _from_row_tiles(ref_at, rows):
    lo, hi = [], []
    for b in range(SUBLANES):
        w = ref_at[pl.ds(b, rows, stride=SUBLANES), :]
        lo.append(pltpu.unpack_elementwise(w, index=0, packed_dtype=BF16, unpacked_dtype=F32))
        hi.append(pltpu.unpack_elementwise(w, index=1, packed_dtype=BF16, unpacked_dtype=F32))
    return jnp.concatenate(lo + hi, axis=1)


def _mods_kernel(c_ref, w_ref, b_ref, o_ref, cond_ref):
    @pl.when((pl.program_id(0) == 0) & (pl.program_id(1) == 0))
    def _():
        c = c_ref[...]
        cond_ref[...] = c * jax.nn.sigmoid(c)

    tn = w_ref.shape[2]
    rows = []
    for b in range(BATCH):
        cb = cond_ref[b]
        cols = [jnp.sum(w_ref[0, :, j * LANES:(j + 1) * LANES] * cb, axis=0, keepdims=True)
                for j in range(tn // LANES)]
        rows.append(jnp.concatenate(cols, axis=1))
    o_ref[0] = jnp.concatenate(rows, axis=0) + b_ref[0]


def _mods(c_lanes, w_mod, b_mod):
    tn = 512
    n3 = 3 * D_MODEL
    return pl.pallas_call(
        _mods_kernel,
        grid=(DEPTH, n3 // tn),
        in_specs=[pl.BlockSpec((BATCH, D_MODEL, LANES), lambda l, j: (0, 0, 0)),
                  pl.BlockSpec((1, D_MODEL, tn), lambda l, j: (l, 0, j)),
                  pl.BlockSpec((1, 1, tn), lambda l, j: (l, 0, j))],
        out_specs=pl.BlockSpec((1, BATCH, tn), lambda l, j: (l, 0, j)),
        out_shape=jax.ShapeDtypeStruct((DEPTH, BATCH, n3), F32),
        scratch_shapes=[pltpu.VMEM((BATCH, D_MODEL, LANES), F32)],
        compiler_params=_cparams(2, 32),
        name="adaln_mods",
    )(c_lanes, w_mod, b_mod.reshape(DEPTH, 1, n3))


def _rope_tables_kernel(pos_ref, fm_ref, sg_ref, fr_ref, cm_ref, sm_ref, cr_ref, sr_ref):
    p = pos_ref[...].astype(F32)
    am = p * fm_ref[...]
    cm_ref[...] = jnp.cos(am)
    sm_ref[...] = jnp.sin(am) * sg_ref[...]
    ar = p * fr_ref[...]
    cr_ref[...] = jnp.cos(ar)
    sr_ref[...] = jnp.sin(ar)


def _rope_tables(pos_lanes):
    half = MLA_ROPE // 2
    f_mla = jnp.exp(-math.log(ROPE_BASE) * jnp.arange(half, dtype=F32) / half)
    z = jnp.zeros((half,), F32)
    fm = jnp.concatenate([f_mla, z, f_mla, z]).reshape(1, LANES)
    sg = jnp.concatenate([-jnp.ones((half,), F32), z, jnp.ones((half,), F32), z]).reshape(1, LANES)
    hr = RET_QK // 2
    fr = jnp.exp(-math.log(ROPE_BASE) * jnp.arange(hr, dtype=F32) / hr).reshape(1, LANES)
    tm = 1024
    row = pl.BlockSpec((tm, LANES), lambda i: (i, 0))
    vec = pl.BlockSpec((1, LANES), lambda i: (0, 0))
    tab = jax.ShapeDtypeStruct((TOKENS, LANES), F32)
    return pl.pallas_call(
        _rope_tables_kernel,
        grid=(TOKENS // tm,),
        in_specs=[row, vec, vec, vec],
        out_specs=[row, row, row, row],
        out_shape=[tab, tab, tab, tab],
        compiler_params=_cparams(1, 32),
        name="rope_tables",
    )(pos_lanes, fm, sg, fr)


def _norm_mod_kernel(x_ref, g_ref, sc_ref, sh_ref, o_ref):
    o_ref[...] = _norm_mod(x_ref[...], g_ref[...], sc_ref[0], sh_ref[0]).astype(o_ref.dtype)


def _norm_mod_rows(x2d, g, scale, shift):
    tm = 512
    per_batch = SEQ // tm
    mod = pl.BlockSpec((1, 1, D_MODEL), lambda i: (i // per_batch, 0, 0))
    row = pl.BlockSpec((tm, D_MODEL), lambda i: (i, 0))
    return pl.pallas_call(
        _norm_mod_kernel,
        grid=(TOKENS // tm,),
        in_specs=[row, pl.BlockSpec((1, D_MODEL), lambda i: (0, 0)), mod, mod],
        out_specs=row,
        out_shape=jax.ShapeDtypeStruct((TOKENS, D_MODEL), BF16),
        compiler_params=_cparams(1, 32),
        name="norm_mod",
    )(x2d, g, scale, shift)


PROJ_TM = 1024


def _proj_kernel(*refs, n_in, residual):
    lhs_refs = refs[:n_in]
    if residual:
        w_ref, x_ref, gate_ref, o_ref, wb_ref = refs[n_in:]
    else:
        w_ref, o_ref, wb_ref = refs[n_in:]

    @pl.when(pl.program_id(1) == 0)
    def _():
        wb_ref[...] = w_ref[...].astype(BF16)

    acc = None
    off = 0
    for y_ref in lhs_refs:
        kk = y_ref.shape[1]
        d = jnp.dot(y_ref[...], wb_ref[off:off + kk, :], preferred_element_type=F32)
        acc = d if acc is None else acc + d
        off += kk
    if residual:
        o_ref[...] = x_ref[...] + gate_ref[0] * acc
    else:
        o_ref[...] = acc.astype(o_ref.dtype)


def _proj(ys, w, tn, vmem_mb, x2d=None, gate=None):
    tm = PROJ_TM
    k, n = w.shape
    per_batch = SEQ // tm
    residual = x2d is not None
    in_specs = [pl.BlockSpec((tm, y.shape[1]), lambda j, i: (i, 0)) for y in ys]
    in_specs.append(pl.BlockSpec((k, tn), lambda j, i: (0, j)))
    args = list(ys) + [w]
    if residual:
        in_specs += [pl.BlockSpec((tm, tn), lambda j, i: (i, j)),
                     pl.BlockSpec((1, 1, tn), lambda j, i: (i // per_batch, 0, j))]
        args += [x2d, gate]
    return pl.pallas_call(
        functools.partial(_proj_kernel, n_in=len(ys), residual=residual),
        grid=(n // tn, TOKENS // tm),
        in_specs=in_specs,
        out_specs=pl.BlockSpec((tm, tn), lambda j, i: (i, j)),
        out_shape=jax.ShapeDtypeStruct((TOKENS, n), F32 if residual else BF16),
        scratch_shapes=[pltpu.VMEM((k, tn), BF16)],
        compiler_params=_cparams(2, vmem_mb),
        name="out_proj" if residual else "in_proj",
    )(*args)


def _mla_prep_kernel(cq_ref, ckv_ref, kr_ref, qn_ref, kvn_ref, wq_ref, wkv_ref, cos_ref, sin_ref,
                     q_ref, k_ref, v_ref):
    cos = cos_ref[...]
    sin = sin_ref[...]

    def rope(x):
        return x * cos + pltpu.roll(x, LANES // 2, 1) * sin

    def rms(x, g):
        xf = x.astype(F32)
        return (xf * lax.rsqrt(jnp.mean(xf * xf, axis=-1, keepdims=True) + EPS)) * g

    cq = rms(cq_ref[...], qn_ref[...]).astype(BF16)
    q = jnp.dot(cq, wq_ref[...].astype(BF16), preferred_element_type=F32)
    ckv = rms(ckv_ref[...], kvn_ref[...]).astype(BF16)
    kv = jnp.dot(ckv, wkv_ref[...].astype(BF16), preferred_element_type=F32)
    kr = rope(kr_ref[...].astype(F32)).astype(BF16)
    scale = (MLA_NOPE + MLA_ROPE) ** -0.5
    for h in range(MLA_HEADS):
        c0 = h * MLA_QK_PAD
        q_ref[:, c0:c0 + MLA_NOPE] = (q[:, c0:c0 + MLA_NOPE] * scale).astype(BF16)
        q_ref[:, c0 + MLA_NOPE:c0 + MLA_QK_PAD] = (
            rope(q[:, c0 + MLA_NOPE:c0 + MLA_QK_PAD]) * scale).astype(BF16)
        k_ref[:, c0:c0 + MLA_NOPE] = kv[:, h * MLA_NOPE:(h + 1) * MLA_NOPE].astype(BF16)
        k_ref[:, c0 + MLA_NOPE:c0 + MLA_QK_PAD] = kr
    v_ref[...] = kv[:, MLA_HEADS * MLA_NOPE:].astype(BF16)


def _mla_prep(proj, q_norm, kv_norm, wq, wkv, cos_m, sin_m):
    tm = 512
    c_q0 = 3 * SB_WIDTH
    c_kv0 = c_q0 + MLA_Q_LORA
    c_kr0 = c_kv0 + MLA_KV_LORA
    full = lambda shape: pl.BlockSpec(shape, lambda i: (0, 0))
    row = lambda w: pl.BlockSpec((tm, w), lambda i: (i, 0))
    qk_w = MLA_HEADS * MLA_QK_PAD
    return pl.pallas_call(
        _mla_prep_kernel,
        grid=(TOKENS // tm,),
        in_specs=[pl.BlockSpec((tm, MLA_Q_LORA), lambda i: (i, c_q0 // MLA_Q_LORA)),
                  pl.BlockSpec((tm, MLA_KV_LORA), lambda i: (i, c_kv0 // MLA_KV_LORA)),
                  pl.BlockSpec((tm, LANES), lambda i: (i, c_kr0 // LANES)),
                  full((1, MLA_Q_LORA)), full((1, MLA_KV_LORA)),
                  full((MLA_Q_LORA, qk_w)), full((MLA_KV_LORA, 2 * MLA_HEADS * MLA_NOPE)),
                  row(LANES), row(LANES)],
        out_specs=[row(qk_w), row(qk_w), row(MLA_HEADS * MLA_V)],
        out_shape=[jax.ShapeDtypeStruct((TOKENS, qk_w), BF16),
                   jax.ShapeDtypeStruct((TOKENS, qk_w), BF16),
                   jax.ShapeDtypeStruct((TOKENS, MLA_HEADS * MLA_V), BF16)],
        compiler_params=_cparams(1, 48),
        name="mla_prep",
    )(proj, proj, proj, q_norm, kv_norm, wq, wkv, cos_m, sin_m)


ATT_T = 256
ATT_HEADS = 2
LOG2E = math.log2(math.e)
SB_SKIP_LOG2 = -160.0


def _sb_attn_kernel(q_ref, k_ref, v_ref, o_ref, acc_ref, run_ref):
    t = ATT_T
    ii = lax.broadcasted_iota(I32, (t, t), 0)
    jj = lax.broadcasted_iota(I32, (t, t), 1)
    strict = jj < ii
    upper2 = (lax.broadcasted_iota(I32, (2 * t, t), 0) % t > lax.broadcasted_iota(I32, (2 * t, t), 1)
              ).astype(BF16)
    z_scale = HEAD_DIM ** -0.5 * LOG2E

    def block(q, kb, cols, run, diag):
        rows = pl.ds(pl.multiple_of(kb * t, t), t)
        z = lax.dot_general(q, k_ref[rows, cols], NT_DIMS, preferred_element_type=F32) * z_scale
        log_beta = jnp.minimum(z, 0.0) - jnp.log2(1.0 + jnp.exp2(jnp.minimum(z, -z)))
        log_keep = log_beta - z
        if diag:
            log_keep = jnp.where(strict, log_keep, 0.0)
        hi = log_keep.astype(BF16)
        lo = (log_keep - hi.astype(F32)).astype(BF16)
        within = jnp.dot(jnp.concatenate([hi, lo], axis=1), upper2, preferred_element_type=F32)
        a = jnp.exp2(log_beta + (within + run))
        if diag:
            a = jnp.where(strict, a, 0.0)
        pv = jnp.dot(a.astype(BF16), v_ref[rows, cols], preferred_element_type=F32)
        return pv, run + (within[:, 0:1] + log_keep[:, 0:1])

    nq = SEQ // t
    for qi in range(nq):
        qrows = pl.ds(qi * t, t)
        for hh in range(ATT_HEADS):
            cols = slice(hh * HEAD_DIM, (hh + 1) * HEAD_DIM)
            q = q_ref[qrows, cols]
            acc, run = block(q, qi, cols, jnp.zeros((t, 1), F32), True)
            if qi > 0:
                pv, run = block(q, qi - 1, cols, run, False)
                acc = acc + pv
            acc_ref[qrows, cols] = acc
            run_ref[qrows, cols] = jnp.broadcast_to(run, (t, HEAD_DIM))

    def more(qrows):
        return (jnp.max(run_ref[qrows, :]) > SB_SKIP_LOG2).astype(I32)

    for qi in range(2, nq):
        qrows = pl.ds(qi * t, t)

        def body(c, qrows=qrows):
            kb = c[0]
            for hh in range(ATT_HEADS):
                cols = slice(hh * HEAD_DIM, (hh + 1) * HEAD_DIM)
                pv, run = block(q_ref[qrows, cols], kb, cols, run_ref[qrows, hh * HEAD_DIM:hh * HEAD_DIM + 1],
                                False)
                acc_ref[qrows, cols] = acc_ref[qrows, cols] + pv
                run_ref[qrows, cols] = jnp.broadcast_to(run, (t, HEAD_DIM))
            return kb - 1, more(qrows)

        lax.while_loop(lambda c: (c[0] >= 0) & (c[1] == 1), body, (jnp.int32(qi - 2), more(qrows)))

    o_ref[...] = acc_ref[...].astype(o_ref.dtype)


def _sb_attention(proj):
    w = ATT_HEADS * HEAD_DIM
    blk = lambda off: pl.BlockSpec((SEQ, w), lambda b, h: (b, off + h))
    n = SB_HEADS // ATT_HEADS
    return pl.pallas_call(
        _sb_attn_kernel,
        grid=(BATCH, n),
        in_specs=[blk(0), blk(n), blk(2 * n)],
        out_specs=pl.BlockSpec((SEQ, w), lambda b, h: (b, h)),
        out_shape=jax.ShapeDtypeStruct((TOKENS, SB_WIDTH), BF16),
        scratch_shapes=[pltpu.VMEM((SEQ, w), F32), pltpu.VMEM((SEQ, w), F32)],
        compiler_params=_cparams(2, 48),
        name="sb_attention",
    )(proj, proj, proj)


def _mla_attn_kernel(q_ref, k_ref, v_ref, o_ref):
    t = ATT_T
    ii = lax.broadcasted_iota(I32, (t, t), 0)
    jj = lax.broadcasted_iota(I32, (t, t), 1)
    causal = jj <= ii
    neg = -1e30

    def step(q, kb, hh, carry, diag):
        m, l, acc = carry
        rows = pl.ds(pl.multiple_of(kb * t, t), t)
        s = lax.dot_general(q, k_ref[rows, hh * MLA_QK_PAD:(hh + 1) * MLA_QK_PAD], NT_DIMS,
                            preferred_element_type=F32)
        if diag:
            s = jnp.where(causal, s, neg)
        m_new = jnp.maximum(m, jnp.max(s, axis=1, keepdims=True))
        alpha = jnp.exp(m - m_new)
        p = jnp.exp(s - m_new)
        l = alpha * l + jnp.sum(p, axis=1, keepdims=True)
        acc = alpha * acc + jnp.dot(p.astype(BF16), v_ref[rows, hh * MLA_V:(hh + 1) * MLA_V],
                                    preferred_element_type=F32)
        return m_new, l, acc

    for qi in range(SEQ // t):
        qrows = pl.ds(qi * t, t)
        for hh in range(ATT_HEADS):
            q = q_ref[qrows, hh * MLA_QK_PAD:(hh + 1) * MLA_QK_PAD]
            c = (jnp.full((t, 1), neg, F32), jnp.zeros((t, 1), F32), jnp.zeros((t, MLA_V), F32))
            for kb in range(qi):
                c = step(q, kb, hh, c, False)
            _, l, acc = step(q, qi, hh, c, True)
            o_ref[qrows, hh * MLA_V:(hh + 1) * MLA_V] = (acc / l).astype(o_ref.dtype)


def _mla_attention(q_cat, k_cat, v):
    qk = pl.BlockSpec((SEQ, ATT_HEADS * MLA_QK_PAD), lambda b, h: (b, h))
    vo = pl.BlockSpec((SEQ, ATT_HEADS * MLA_V), lambda b, h: (b, h))
    return pl.pallas_call(
        _mla_attn_kernel,
        grid=(BATCH, MLA_HEADS // ATT_HEADS),
        in_specs=[qk, qk, vo],
        out_specs=vo,
        out_shape=jax.ShapeDtypeStruct((TOKENS, MLA_HEADS * MLA_V), BF16),
        compiler_params=_cparams(2, 48),
        name="mla_attention",
    )(q_cat, k_cat, v)


def _retention_kernel(lg_ref, q_ref, k_ref, v_ref, g_ref, cos_ref, sin_ref, o_ref, state_ref):
    lg = lg_ref[pl.program_id(1)]
    c = CHUNK
    half = RET_QK // 2
    ii = lax.broadcasted_iota(I32, (c, c), 0)
    jj = lax.broadcasted_iota(I32, (c, c), 1)
    rel = (ii - jj).astype(F32)
    intra = jnp.where(rel >= 0, jnp.exp(jnp.maximum(rel, 0.0) * lg), 0.0)
    idx = lax.broadcasted_iota(I32, (c, 1), 0).astype(F32)
    q_decay = jnp.exp((idx + 1.0) * lg)
    k_decay = jnp.exp((c - 1.0 - idx) * lg)
    chunk_decay = jnp.exp(jnp.full((1, 1), float(c), F32) * lg)
    state_ref[...] = jnp.zeros_like(state_ref)

    def rope(x, cos, sin):
        x1, x2 = x[:, :half], x[:, half:]
        return jnp.concatenate([x1 * cos - x2 * sin, x1 * sin + x2 * cos], axis=1)

    for ci in range(SEQ // c):
        rows = pl.ds(ci * c, c)
        cos = cos_ref[rows, :]
        sin = sin_ref[rows, :]
        qr = rope(q_ref[rows, :].astype(F32), cos, sin)
        kr = rope(k_ref[rows, :].astype(F32), cos, sin) * (RET_QK ** -0.5)
        v = v_ref[rows, :]
        qb = qr.astype(BF16)
        scores = lax.dot_general(qb, kr.astype(BF16), NT_DIMS, preferred_element_type=F32) * intra
        inner = jnp.dot(scores.astype(BF16), v, preferred_element_type=F32)
        st = state_ref[...]
        cross = jnp.dot(qb, st.astype(BF16), preferred_element_type=F32) * q_decay
        kd_t = jnp.transpose(kr * k_decay).astype(BF16)
        state_ref[...] = st * chunk_decay + jnp.dot(kd_t, v, preferred_element_type=F32)
        o = inner + cross
        o = o * lax.rsqrt(jnp.mean(o * o, axis=-1, keepdims=True) + EPS)
        gg = g_ref[rows, :].astype(F32)
        o_ref[rows, :] = ((gg * jax.nn.sigmoid(gg)) * o).astype(o_ref.dtype)


def _retention(proj, cos_r, sin_r):
    log_gamma = jnp.log1p(-jnp.exp2(-5.0 - jnp.arange(RET_HEADS, dtype=F32)))
    k0 = RET_QK_WIDTH // RET_QK
    v0 = 2 * RET_QK_WIDTH // RET_V
    g0 = (2 * RET_QK_WIDTH + RET_V_WIDTH) // RET_V
    tab = pl.BlockSpec((SEQ, LANES), lambda b, h: (b, 0))
    return pl.pallas_call(
        _retention_kernel,
        grid=(BATCH, RET_HEADS),
        in_specs=[pl.BlockSpec(memory_space=pltpu.SMEM),
                  pl.BlockSpec((SEQ, RET_QK), lambda b, h: (b, h)),
                  pl.BlockSpec((SEQ, RET_QK), lambda b, h: (b, k0 + h)),
                  pl.BlockSpec((SEQ, RET_V), lambda b, h: (b, v0 + h)),
                  pl.BlockSpec((SEQ, RET_V), lambda b, h: (b, g0 + h)),
                  tab, tab],
        out_specs=pl.BlockSpec((SEQ, RET_V), lambda b, h: (b, h)),
        out_shape=jax.ShapeDtypeStruct((TOKENS, RET_V_WIDTH), BF16),
        scratch_shapes=[pltpu.VMEM((RET_QK, RET_V), F32)],
        compiler_params=_cparams(2, 48),
        name="retention",
    )(log_gamma, proj, proj, proj, proj, cos_r, sin_r)


ROUTER_TM = 512


def _router_kernel(x_ref, g_ref, sc_ref, sh_ref, w_ref, b_ref, ei_ref, wcol_ref, cnt_ref, carry_ref):
    tm = ROUTER_TM

    @pl.when(pl.program_id(0) == 0)
    def _():
        carry_ref[...] = jnp.zeros_like(carry_ref)

    def split(a):
        hi = a.astype(BF16)
        return hi, (a - hi.astype(F32)).astype(BF16)

    h = _norm_mod(x_ref[...], g_ref[...], sc_ref[0], sh_ref[0])
    h_hi, h_lo = split(h)
    w_hi, w_lo = split(w_ref[...])
    nt = lambda a, b: lax.dot_general(a, b, NT_DIMS, preferred_element_type=F32)
    logits = (nt(w_hi, h_hi) + nt(w_hi, h_lo) + nt(w_lo, h_hi)) + b_ref[...]
    e_log = logits[0:N_EXPERTS]
    g_log = logits[N_EXPERTS:N_EXPERTS + N_GROUPS]

    def top1(vals, n):
        rows = lax.broadcasted_iota(I32, (n, tm), 0)
        m = jnp.max(vals, axis=0, keepdims=True)
        return m, jnp.min(jnp.where(vals == m, rows, n), axis=0, keepdims=True), rows

    g_max, g_idx, _ = top1(g_log, N_GROUPS)
    g_w = 1.0 / jnp.sum(jnp.exp(g_log - g_max), axis=0, keepdims=True)
    sel = e_log[0:EXPERTS_PER_GROUP]
    for gi in range(1, N_GROUPS):
        sel = jnp.where(g_idx == gi, e_log[gi * EXPERTS_PER_GROUP:(gi + 1) * EXPERTS_PER_GROUP], sel)
    m1, i1, rows8 = top1(sel, EXPERTS_PER_GROUP)
    m2, i2, _ = top1(jnp.where(rows8 == i1, -jnp.inf, sel), EXPERTS_PER_GROUP)
    ratio = jnp.exp(m2 - m1)
    w1 = g_w / (1.0 + ratio)
    w2 = (g_w * ratio) / (1.0 + ratio)
    e1 = g_idx * EXPERTS_PER_GROUP + i1
    e2 = g_idx * EXPERTS_PER_GROUP + i2

    rows32 = lax.broadcasted_iota(I32, (N_EXPERTS, tm), 0)
    hit1 = rows32 == e1
    hit2 = rows32 == e2
    onehot = jnp.where(hit1 | hit2, 1.0, 0.0)
    jj = lax.broadcasted_iota(I32, (tm, tm), 0)
    tt = lax.broadcasted_iota(I32, (tm, tm), 1)
    before = (jj < tt).astype(BF16)
    rank_e = jnp.dot(onehot.astype(BF16), before, preferred_element_type=F32) + carry_ref[:, 0:1]
    r1 = jnp.sum(jnp.where(hit1, rank_e, 0.0), axis=0, keepdims=True)
    r2 = jnp.sum(jnp.where(hit2, rank_e, 0.0), axis=0, keepdims=True)
    carry_ref[...] = carry_ref[...] + jnp.sum(onehot, axis=1, keepdims=True)
    cnt_ref[...] = carry_ref[...]

    ei_ref[...] = jnp.where(rows8 == 0, e1, jnp.where(rows8 == 1, e2, jnp.where(
        rows8 == 2, r1.astype(I32), jnp.where(rows8 == 3, r2.astype(I32), 0))))
    rows128 = lax.broadcasted_iota(I32, (LANES, tm), 0)
    wrows = jnp.where(rows128 == 0, w1, jnp.where(rows128 == 1, w2, 0.0))
    wcol_ref[...] = jnp.transpose(wrows)


def _router(x2d, g, scale, shift, w_t, b_rows):
    tm = ROUTER_TM
    per_batch = SEQ // tm
    mod = pl.BlockSpec((1, 1, D_MODEL), lambda i: (i // per_batch, 0, 0))
    return pl.pallas_call(
        _router_kernel,
        grid=(TOKENS // tm,),
        in_specs=[pl.BlockSpec((tm, D_MODEL), lambda i: (i, 0)),
                  pl.BlockSpec((1, D_MODEL), lambda i: (0, 0)),
                  mod, mod,
                  pl.BlockSpec((ROUTER_ROWS, D_MODEL), lambda i: (0, 0)),
                  pl.BlockSpec((ROUTER_ROWS, tm), lambda i: (0, 0))],
        out_specs=[pl.BlockSpec((SUBLANES, tm), lambda i: (0, i)),
                   pl.BlockSpec((tm, LANES), lambda i: (i, 0)),
                   pl.BlockSpec((N_EXPERTS, LANES), lambda i: (0, 0))],
        out_shape=[jax.ShapeDtypeStruct((SUBLANES, TOKENS), I32),
                   jax.ShapeDtypeStruct((TOKENS, LANES), F32),
                   jax.ShapeDtypeStruct((N_EXPERTS, LANES), F32)],
        scratch_shapes=[pltpu.VMEM((N_EXPERTS, LANES), F32)],
        compiler_params=_cparams(1, 48),
        name="moe_router",
    )(x2d, g, scale, shift, w_t, b_rows)


DISPATCH_TM = 256


def _row_copy(src, dst, sem):
    return pltpu.make_async_copy(src, dst, sem)


def _dispatch_kernel(pos0_ref, pos1_ref, meta_ref, x_ref, g_ref, sc_ref, sh_ref, xs_ref,
                     buf_ref, zero_ref, sem, zsem):
    tm = DISPATCH_TM
    i = pl.program_id(0)
    n = pl.num_programs(0)
    slot = i % 2

    def clear_padding(act):
        def zero_copy(row, nrows):
            return _row_copy(zero_ref.at[pl.ds(0, nrows * SUBLANES), :],
                             xs_ref.at[pl.ds(pl.multiple_of(row * SUBLANES, SUBLANES), nrows * SUBLANES), :],
                             zsem)

        def per_expert(e, carry):
            cnt = meta_ref[e]
            row = meta_ref[N_EXPERTS + e] + cnt
            pad = (-cnt) & (MOE_TM - 1)
            for bit in (128, 64, 32, 16, 8, 4, 2, 1):
                @pl.when((pad & bit) != 0)
                def _(row=row, bit=bit):
                    act(zero_copy(row, bit))
                row = row + (pad & bit)
            return carry

        lax.fori_loop(0, N_EXPERTS, per_expert, 0)

        def per_tile(tl, carry):
            for part in range(MOE_TM // ZERO_ROWS):
                act(zero_copy(tl * MOE_TM + part * ZERO_ROWS, ZERO_ROWS))
            return carry

        lax.fori_loop(meta_ref[2 * N_EXPERTS], MOE_TILES, per_tile, 0)

    @pl.when(i == 0)
    def _():
        zero_ref[...] = jnp.zeros_like(zero_ref)
        clear_padding(lambda cp: cp.start())

    def wait_slot(s):
        for _ in range(2):
            _row_copy(buf_ref.at[s], xs_ref.at[pl.ds(0, tm * SUBLANES), :], sem.at[s]).wait()

    @pl.when(i >= 2)
    def _():
        wait_slot(slot)

    h = _norm_mod(x_ref[...], g_ref[...], sc_ref[0], sh_ref[0])
    _to_row_tiles(buf_ref.at[slot], h, tm)

    def issue(r, carry):
        tok = i * tm + r
        src = buf_ref.at[slot, pl.ds(pl.multiple_of(r * SUBLANES, SUBLANES), SUBLANES), :]
        for pos_ref in (pos0_ref, pos1_ref):
            row = pl.multiple_of(pos_ref[tok] * SUBLANES, SUBLANES)
            _row_copy(src, xs_ref.at[pl.ds(row, SUBLANES), :], sem.at[slot]).start()
        return carry

    lax.fori_loop(0, tm, issue, 0)

    @pl.when(i == n - 1)
    def _():
        wait_slot(1 - slot)
        wait_slot(slot)
        clear_padding(lambda cp: cp.wait())


def _dispatch(pos0, pos1, meta, x2d, g, scale, shift):
    tm = DISPATCH_TM
    per_batch = SEQ // tm
    mod = pl.BlockSpec((1, 1, D_MODEL), lambda i, *_: (i // per_batch, 0, 0))
    grid_spec = pltpu.PrefetchScalarGridSpec(
        num_scalar_prefetch=3,
        grid=(TOKENS // tm,),
        in_specs=[pl.BlockSpec((tm, D_MODEL), lambda i, *_: (i, 0)),
                  pl.BlockSpec((1, D_MODEL), lambda i, *_: (0, 0)),
                  mod, mod],
        out_specs=pl.BlockSpec(memory_space=pl.ANY),
        scratch_shapes=[pltpu.VMEM((2, tm * SUBLANES, LANES), U32),
                        pltpu.VMEM((ZERO_ROWS * SUBLANES, LANES), U32),
                        pltpu.SemaphoreType.DMA((2,)),
                        pltpu.SemaphoreType.DMA(())])
    return pl.pallas_call(
        _dispatch_kernel,
        grid_spec=grid_spec,
        out_shape=jax.ShapeDtypeStruct((MOE_ROWS * SUBLANES, LANES), U32),
        compiler_params=_cparams(1, 48),
        name="moe_dispatch",
    )(pos0, pos1, meta, x2d, g, scale, shift)


def _experts_kernel(te_ref, tf_ref, tv_ref, ti_ref, xs_ref, wg_ref, wu_ref, wd_ref, ys_ref,
                    wgb_ref, wub_ref, wdb_ref):
    del te_ref, ti_ref
    tm = MOE_TM
    j = pl.program_id(0)

    @pl.when(tf_ref[j] == 1)
    def _():
        wgb_ref[...] = wg_ref[0, 0].astype(BF16)
        wub_ref[...] = wu_ref[0, 0].astype(BF16)
        wdb_ref[...] = wd_ref[0, 0].astype(BF16)

    @pl.when(tv_ref[j] == 1)
    def _():
        x = _from_row_tiles(xs_ref, tm).astype(BF16)
        gate = jnp.dot(x, wgb_ref[...], preferred_element_type=F32)
        up = jnp.dot(x, wub_ref[...], preferred_element_type=F32)
        a = ((gate * jax.nn.sigmoid(gate)) * up).astype(BF16)
        _to_row_tiles(ys_ref, jnp.dot(a, wdb_ref[...], preferred_element_type=F32), tm)

    @pl.when(tv_ref[j] == 0)
    def _():
        ys_ref[...] = jnp.zeros_like(ys_ref)


def _experts(plan, xs, w_gate, w_up, w_down, layer):
    tm = MOE_TM
    rows = pl.BlockSpec((tm * SUBLANES, LANES), lambda j, te, tf, tv, ti: (ti[j], 0))
    rows_out = pl.BlockSpec((tm * SUBLANES, LANES), lambda j, te, tf, tv, ti: (j, 0))
    w_in = pl.BlockSpec((1, 1, D_MODEL, EXPERT_HIDDEN), lambda j, te, tf, tv, ti: (layer, te[j], 0, 0))
    w_out = pl.BlockSpec((1, 1, EXPERT_HIDDEN, D_MODEL), lambda j, te, tf, tv, ti: (layer, te[j], 0, 0))
    grid_spec = pltpu.PrefetchScalarGridSpec(
        num_scalar_prefetch=4,
        grid=(MOE_TILES,),
        in_specs=[rows, w_in, w_in, w_out],
        out_specs=rows_out,
        scratch_shapes=[pltpu.VMEM((D_MODEL, EXPERT_HIDDEN), BF16),
                        pltpu.VMEM((D_MODEL, EXPERT_HIDDEN), BF16),
                        pltpu.VMEM((EXPERT_HIDDEN, D_MODEL), BF16)])
    return pl.pallas_call(
        _experts_kernel,
        grid_spec=grid_spec,
        out_shape=jax.ShapeDtypeStruct((MOE_ROWS * SUBLANES, LANES), U32),
        compiler_params=_cparams(1, 58),
        name="moe_experts",
    )(*plan, xs, w_gate, w_up, w_down)


COMBINE_TM = 256


def _combine_kernel(pos0_ref, pos1_ref, x_ref, gate_ref, wcol_ref, *refs, final):
    if final:
        fg_ref, ys_ref, o_ref, buf_ref, sem = refs
    else:
        ng_ref, nsc_ref, nsh_ref, ys_ref, o_ref, h_ref, buf_ref, sem = refs
    tm = COMBINE_TM
    i = pl.program_id(0)
    n = pl.num_programs(0)
    slot = i % 2

    def issue(step, s):
        def body(r, carry):
            tok = step * tm + r
            dst_rows = pl.ds(pl.multiple_of(r * SUBLANES, SUBLANES), SUBLANES)
            for k, pos_ref in enumerate((pos0_ref, pos1_ref)):
                row = pl.multiple_of(pos_ref[tok] * SUBLANES, SUBLANES)
                _row_copy(ys_ref.at[pl.ds(row, SUBLANES), :], buf_ref.at[s, k, dst_rows, :],
                          sem.at[s]).start()
            return carry

        lax.fori_loop(0, tm, body, 0)

    @pl.when(i == 0)
    def _():
        issue(0, 0)

    @pl.when(i + 1 < n)
    def _():
        issue(i + 1, 1 - slot)

    for k in range(2):
        _row_copy(ys_ref.at[pl.ds(0, tm * SUBLANES), :], buf_ref.at[slot, k], sem.at[slot]).wait()

    y0 = _from_row_tiles(buf_ref.at[slot, 0], tm)
    y1 = _from_row_tiles(buf_ref.at[slot, 1], tm)
    out = x_ref[...] + gate_ref[0] * (wcol_ref[:, 0:1] * y0 + wcol_ref[:, 1:2] * y1)
    if final:
        o_ref[...] = (out * lax.rsqrt(jnp.mean(out * out, axis=-1, keepdims=True) + EPS)) * fg_ref[...]
    else:
        o_ref[...] = out
        h_ref[...] = _norm_mod(out, ng_ref[...], nsc_ref[0], nsh_ref[0]).astype(h_ref.dtype)


def _combine(pos0, pos1, x2d, gate, wcol, ys, final_g=None, next_norm=None):
    tm = COMBINE_TM
    per_batch = SEQ // tm
    final = final_g is not None
    row = pl.BlockSpec((tm, D_MODEL), lambda i, p0, p1: (i, 0))
    vec = pl.BlockSpec((1, D_MODEL), lambda i, p0, p1: (0, 0))
    mod = pl.BlockSpec((1, 1, D_MODEL), lambda i, p0, p1: (i // per_batch, 0, 0))
    extra_specs, extra = ([vec], [final_g]) if final else ([vec, mod, mod], list(next_norm))
    out_f32 = jax.ShapeDtypeStruct((TOKENS, D_MODEL), F32)
    grid_spec = pltpu.PrefetchScalarGridSpec(
        num_scalar_prefetch=2,
        grid=(TOKENS // tm,),
        in_specs=[row, mod, pl.BlockSpec((tm, LANES), lambda i, p0, p1: (i, 0))] + extra_specs
        + [pl.BlockSpec(memory_space=pl.ANY)],
        out_specs=row if final else [row, row],
        scratch_shapes=[pltpu.VMEM((2, 2, tm * SUBLANES, LANES), U32),
                        pltpu.SemaphoreType.DMA((2,))])
    return pl.pallas_call(
        functools.partial(_combine_kernel, final=final),
        grid_spec=grid_spec,
        out_shape=out_f32 if final else [out_f32, jax.ShapeDtypeStruct((TOKENS, D_MODEL), BF16)],
        compiler_params=_cparams(1, 48),
        name="moe_combine",
    )(pos0, pos1, x2d, gate, wcol, *extra, ys)


def _moe_plan(counts):
    counts = counts.astype(I32)
    tiles = (counts + (MOE_TM - 1)) // MOE_TM
    cum = jnp.cumsum(tiles)
    first_tile = cum - tiles
    n_used = cum[-1]
    j = jnp.arange(MOE_TILES, dtype=I32)
    te = jnp.minimum(jnp.sum((j[:, None] >= cum[None, :]).astype(I32), axis=1), N_EXPERTS - 1)
    valid = j < n_used
    te = jnp.where(valid, te, jnp.take(te, n_used - 1))
    tf = (valid & (j == jnp.take(first_tile, te))).astype(I32)
    ti = jnp.where(valid, j, n_used - 1)
    row_off = first_tile * MOE_TM
    meta = jnp.concatenate([counts, row_off, n_used.reshape(1)])
    return (te, tf, valid.astype(I32), ti), row_off, meta


def _positions_kernel(ei_ref, off_ref, pos_ref):
    tm = ei_ref.shape[1]
    rows32 = lax.broadcasted_iota(I32, (N_EXPERTS, tm), 0)
    rows8 = lax.broadcasted_iota(I32, (SUBLANES, tm), 0)
    off = off_ref[:, 0:1]

    def pos(k):
        first = jnp.sum(jnp.where(rows32 == ei_ref[k:k + 1, :], off, 0), axis=0, keepdims=True)
        return first + ei_ref[k + 2:k + 3, :]

    pos_ref[...] = jnp.where(rows8 == 0, pos(0), jnp.where(rows8 == 1, pos(1), 0))


def _positions(ei, row_off):
    tm = 2048
    blk = pl.BlockSpec((SUBLANES, tm), lambda i: (0, i))
    return pl.pallas_call(
        _positions_kernel,
        grid=(TOKENS // tm,),
        in_specs=[blk, pl.BlockSpec((N_EXPERTS, LANES), lambda i: (0, 0))],
        out_specs=blk,
        out_shape=jax.ShapeDtypeStruct((SUBLANES, TOKENS), I32),
        compiler_params=_cparams(1, 32),
        name="moe_positions",
    )(ei, jnp.broadcast_to(row_off[:, None], (N_EXPERTS, LANES)))


def _moe_layer(x2d, layer, mods, norm_g, w_group, b_group, w_expert, b_expert, w_gate, w_up, w_down,
               final_g=None, next_norm=None):
    shift, scale, gate = mods
    w_t = jnp.concatenate([w_expert.T, w_group.T,
                           jnp.zeros((ROUTER_ROWS - N_EXPERTS - N_GROUPS, D_MODEL), F32)], axis=0)
    b_rows = jnp.concatenate([b_expert, b_group, jnp.zeros((ROUTER_ROWS - N_EXPERTS - N_GROUPS,), F32)])
    b_rows = jnp.broadcast_to(b_rows[:, None], (ROUTER_ROWS, ROUTER_TM))
    ei, wcol, cnt = _router(x2d, norm_g, scale, shift, w_t, b_rows)
    plan, row_off, meta = _moe_plan(cnt[:, 0])
    pos = _positions(ei, row_off)
    pos0, pos1 = pos[0], pos[1]
    xs = _dispatch(pos0, pos1, meta, x2d, norm_g, scale, shift)
    ys = _experts(plan, xs, w_gate, w_up, w_down, layer)
    return _combine(pos0, pos1, x2d, gate, wcol, ys, final_g=final_g, next_norm=next_norm)


def _split_mods(mod, layer):
    m = mod[layer]
    return tuple(m[:, k * D_MODEL:(k + 1) * D_MODEL].reshape(BATCH, 1, D_MODEL) for k in range(3))


def kernel(x, c, positions, w_mod_mix, b_mod_mix, norm_mix, w_mod_ffn, b_mod_ffn, norm_ffn, ev_w_in, ev_q_norm, ev_w_q_up, ev_kv_norm, ev_w_kv_up, ev_w_out, od_w_in, od_w_out, moe_w_group, moe_b_group, moe_w_expert, moe_b_expert, moe_w_gate, moe_w_up, moe_w_down, final_norm):
    x2d = x.reshape(TOKENS, D_MODEL)
    c_lanes = jnp.broadcast_to(c[:, :, None], (BATCH, D_MODEL, LANES))
    mod_mix = _mods(c_lanes, w_mod_mix, b_mod_mix)
    mod_ffn = _mods(c_lanes, w_mod_ffn, b_mod_ffn)
    pos_lanes = jnp.broadcast_to(positions.reshape(TOKENS, 1), (TOKENS, LANES))
    cos_m, sin_m, cos_r, sin_r = _rope_tables(pos_lanes)
    final_g = final_norm.reshape(1, D_MODEL)

    shift, scale, gate = _split_mods(mod_mix, 0)
    half = MLA_ROPE // 2
    w_in = ev_w_in[0]
    c_kr0 = 3 * SB_WIDTH + MLA_Q_LORA + MLA_KV_LORA
    zc = lambda n: jnp.zeros((D_MODEL, n), F32)
    w_in_pad = jnp.concatenate([w_in[:, :c_kr0], w_in[:, c_kr0:c_kr0 + half], zc(half),
                                w_in[:, c_kr0 + half:], zc(half), zc(LANES)], axis=1).astype(BF16)
    h = _norm_mod_rows(x2d, norm_mix[0].reshape(1, D_MODEL), scale, shift)
    proj = _proj([h], w_in_pad, 1024, 48)
    wq = ev_w_q_up[0].reshape(MLA_Q_LORA, MLA_HEADS, MLA_NOPE + MLA_ROPE)
    zq = jnp.zeros((MLA_Q_LORA, MLA_HEADS, half), F32)
    wq = jnp.concatenate([wq[:, :, :MLA_NOPE], wq[:, :, MLA_NOPE:MLA_NOPE + half], zq,
                          wq[:, :, MLA_NOPE + half:], zq], axis=2).reshape(MLA_Q_LORA, MLA_HEADS * MLA_QK_PAD)
    wkv = ev_w_kv_up[0].reshape(MLA_KV_LORA, MLA_HEADS, MLA_NOPE + MLA_V)
    wkv = jnp.concatenate([wkv[:, :, :MLA_NOPE].reshape(MLA_KV_LORA, MLA_HEADS * MLA_NOPE),
                           wkv[:, :, MLA_NOPE:].reshape(MLA_KV_LORA, MLA_HEADS * MLA_V)], axis=1)
    q_cat, k_cat, v_mla = _mla_prep(proj, ev_q_norm[0].reshape(1, MLA_Q_LORA),
                                    ev_kv_norm[0].reshape(1, MLA_KV_LORA), wq, wkv, cos_m, sin_m)
    o_sb = _sb_attention(proj)
    o_mla = _mla_attention(q_cat, k_cat, v_mla)
    x2d = _proj([o_sb, o_mla], ev_w_out[0], 512, 40, x2d, gate)
    shift, scale, gate = _split_mods(mod_mix, 1)
    x2d, h = _moe_layer(x2d, 0, _split_mods(mod_ffn, 0), norm_ffn[0].reshape(1, D_MODEL),
                        moe_w_group[0], moe_b_group[0], moe_w_expert[0], moe_b_expert[0],
                        moe_w_gate, moe_w_up, moe_w_down,
                        next_norm=(norm_mix[1].reshape(1, D_MODEL), scale, shift))

    proj = _proj([h], od_w_in[0], 1024, 48)
    o_ret = _retention(proj, cos_r, sin_r)
    x2d = _proj([o_ret], od_w_out[0], 512, 56, x2d, gate)
    out = _moe_layer(x2d, 1, _split_mods(mod_ffn, 1), norm_ffn[1].reshape(1, D_MODEL),
                     moe_w_group[1], moe_b_group[1], moe_w_expert[1], moe_b_expert[1],
                     moe_w_gate, moe_w_up, moe_w_down, final_g=final_g)
    return out.reshape(BATCH, SEQ, D_MODEL)
```

```python
import functools
import math

import jax
import jax.numpy as jnp
from jax import lax
from jax.experimental import pallas as pl
from jax.experimental.pallas import tpu as pltpu

F32 = jnp.float32
BF16 = jnp.bfloat16
I32 = jnp.int32

D_MODEL = 2048
BATCH = 4
SEQ = 2048
DEPTH = 2
TOKENS = BATCH * SEQ

HEAD_DIM = 128
SB_HEADS = 8
SB_WIDTH = SB_HEADS * HEAD_DIM
MLA_HEADS = 8
MLA_Q_LORA = 512
MLA_KV_LORA = 256
MLA_NOPE = 128
MLA_ROPE = 64
MLA_V = 128
MLA_QK_PAD = 256
EVEN_IN_PAD = 4096
RET_HEADS = 8
RET_QK = 256
RET_V = 512
RET_QK_WIDTH = RET_HEADS * RET_QK
RET_V_WIDTH = RET_HEADS * RET_V
ODD_IN_WIDTH = 2 * RET_QK_WIDTH + 2 * RET_V_WIDTH
N_GROUPS = 4
EXPERTS_PER_GROUP = 8
N_EXPERTS = N_GROUPS * EXPERTS_PER_GROUP
EXPERT_HIDDEN = 512
CHUNK = 128
ROPE_BASE = 10000.0
EPS = 1e-6

LANES = 128
SUBLANES = 8
HALF_D = D_MODEL // 2
assert HALF_D == SUBLANES * LANES
U32 = jnp.uint32

ROUTER_ROWS = 48
MOE_TM = 256
MOE_TILES = (2 * TOKENS) // MOE_TM + N_EXPERTS
MOE_ROWS = MOE_TILES * MOE_TM
ZERO_ROWS = MOE_TM // 2

NT_DIMS = (((1,), (1,)), ((), ()))


def _cparams(n_grid, vmem_mb):
    return pltpu.CompilerParams(dimension_semantics=("arbitrary",) * n_grid,
                                vmem_limit_bytes=vmem_mb * 1024 * 1024)


def _norm_mod(x, g, scale, shift):
    y = x * lax.rsqrt(jnp.mean(x * x, axis=-1, keepdims=True) + EPS)
    return (y * g) * (1.0 + scale) + shift


def _to_row_tiles(ref_at, val, rows):
    for b in range(SUBLANES):
        lo = val[:, b * LANES:(b + 1) * LANES]
        hi = val[:, HALF_D + b * LANES:HALF_D + (b + 1) * LANES]
        ref_at[pl.ds(b, rows, stride=SUBLANES), :] = pltpu.pack_elementwise([lo, hi], packed_dtype=BF16)


def _from_row_tiles(ref_at, rows):
    lo, hi = [], []
    for b in range(SUBLANES):
        w = ref_at[pl.ds(b, rows, stride=SUBLANES), :]
        lo.append(pltpu.unpack_elementwise(w, index=0, packed_dtype=BF16, unpacked_dtype=F32))
        hi.append(pltpu.unpack_elementwise(w, index=1, packed_dtype=BF16, unpacked_dtype=F32))
    return jnp.concatenate(lo + hi, axis=1)


def _mods_kernel(c_ref, w_ref, b_ref, o_ref, cond_ref):
    @pl.when((pl.program_id(0) == 0) & (pl.program_id(1) == 0))
    def _():
        c = c_ref[...]
        cond_ref[...] = c * jax.nn.sigmoid(c)

    tn = w_ref.shape[2]
    rows = []
    for b in range(BATCH):
        cb = cond_ref[b]
        cols = [jnp.sum(w_ref[0, :, j * LANES:(j + 1) * LANES] * cb, axis=0, keepdims=True)
                for j in range(tn // LANES)]
        rows.append(jnp.concatenate(cols, axis=1))
    o_ref[0] = jnp.concatenate(rows, axis=0) + b_ref[0]


def _mods(c_lanes, w_mod, b_mod):
    tn = 512
    n3 = 3 * D_MODEL
    return pl.pallas_call(
        _mods_kernel,
        grid=(DEPTH, n3 // tn),
        in_specs=[pl.BlockSpec((BATCH, D_MODEL, LANES), lambda l, j: (0, 0, 0)),
                  pl.BlockSpec((1, D_MODEL, tn), lambda l, j: (l, 0, j)),
                  pl.BlockSpec((1, 1, tn), lambda l, j: (l, 0, j))],
        out_specs=pl.BlockSpec((1, BATCH, tn), lambda l, j: (l, 0, j)),
        out_shape=jax.ShapeDtypeStruct((DEPTH, BATCH, n3), F32),
        scratch_shapes=[pltpu.VMEM((BATCH, D_MODEL, LANES), F32)],
        compiler_params=_cparams(2, 32),
        name="adaln_mods",
    )(c_lanes, w_mod, b_mod.reshape(DEPTH, 1, n3))


def _rope_tables_kernel(pos_ref, fm_ref, sg_ref, fr_ref, cm_ref, sm_ref, cr_ref, sr_ref):
    p = pos_ref[...].astype(F32)
    am = p * fm_ref[...]
    cm_ref[...] = jnp.cos(am)
    sm_ref[...] = jnp.sin(am) * sg_ref[...]
    ar = p * fr_ref[...]
    cr_ref[...] = jnp.cos(ar)
    sr_ref[...] = jnp.sin(ar)


def _rope_tables(pos_lanes):
    half = MLA_ROPE // 2
    f_mla = jnp.exp(-math.log(ROPE_BASE) * jnp.arange(half, dtype=F32) / half)
    z = jnp.zeros((half,), F32)
    fm = jnp.concatenate([f_mla, z, f_mla, z]).reshape(1, LANES)
    sg = jnp.concatenate([-jnp.ones((half,), F32), z, jnp.ones((half,), F32), z]).reshape(1, LANES)
    hr = RET_QK // 2
    fr = jnp.exp(-math.log(ROPE_BASE) * jnp.arange(hr, dtype=F32) / hr).reshape(1, LANES)
    tm = 1024
    row = pl.BlockSpec((tm, LANES), lambda i: (i, 0))
    vec = pl.BlockSpec((1, LANES), lambda i: (0, 0))
    tab = jax.ShapeDtypeStruct((TOKENS, LANES), F32)
    return pl.pallas_call(
        _rope_tables_kernel,
        grid=(TOKENS // tm,),
        in_specs=[row, vec, vec, vec],
        out_specs=[row, row, row, row],
        out_shape=[tab, tab, tab, tab],
        compiler_params=_cparams(1, 32),
        name="rope_tables",
    )(pos_lanes, fm, sg, fr)


def _norm_mod_kernel(x_ref, g_ref, sc_ref, sh_ref, o_ref):
    o_ref[...] = _norm_mod(x_ref[...], g_ref[...], sc_ref[0], sh_ref[0]).astype(o_ref.dtype)


def _norm_mod_rows(x2d, g, scale, shift):
    tm = 512
    per_batch = SEQ // tm
    mod = pl.BlockSpec((1, 1, D_MODEL), lambda i: (i // per_batch, 0, 0))
    row = pl.BlockSpec((tm, D_MODEL), lambda i: (i, 0))
    return pl.pallas_call(
        _norm_mod_kernel,
        grid=(TOKENS // tm,),
        in_specs=[row, pl.BlockSpec((1, D_MODEL), lambda i: (0, 0)), mod, mod],
        out_specs=row,
        out_shape=jax.ShapeDtypeStruct((TOKENS, D_MODEL), BF16),
        compiler_params=_cparams(1, 32),
        name="norm_mod",
    )(x2d, g, scale, shift)


PROJ_TM = 1024


def _proj_kernel(*refs, n_in, residual):
    lhs_refs = refs[:n_in]
    if residual:
        w_ref, x_ref, gate_ref, o_ref, wb_ref = refs[n_in:]
    else:
        w_ref, o_ref, wb_ref = refs[n_in:]

    @pl.when(pl.program_id(1) == 0)
    def _():
        wb_ref[...] = w_ref[...].astype(BF16)

    acc = None
    off = 0
    for y_ref in lhs_refs:
        kk = y_ref.shape[1]
        d = jnp.dot(y_ref[...], wb_ref[off:off + kk, :], preferred_element_type=F32)
        acc = d if acc is None else acc + d
        off += kk
    if residual:
        o_ref[...] = x_ref[...] + gate_ref[0] * acc
    else:
        o_ref[...] = acc.astype(o_ref.dtype)


def _proj(ys, w, tn, vmem_mb, x2d=None, gate=None):
    tm = PROJ_TM
    k, n = w.shape
    per_batch = SEQ // tm
    residual = x2d is not None
    in_specs = [pl.BlockSpec((tm, y.shape[1]), lambda j, i: (i, 0)) for y in ys]
    in_specs.append(pl.BlockSpec((k, tn), lambda j, i: (0, j)))
    args = list(ys) + [w]
    if residual:
        in_specs += [pl.BlockSpec((tm, tn), lambda j, i: (i, j)),
                     pl.BlockSpec((1, 1, tn), lambda j, i: (i // per_batch, 0, j))]
        args += [x2d, gate]
    return pl.pallas_call(
        functools.partial(_proj_kernel, n_in=len(ys), residual=residual),
        grid=(n // tn, TOKENS // tm),
        in_specs=in_specs,
        out_specs=pl.BlockSpec((tm, tn), lambda j, i: (i, j)),
        out_shape=jax.ShapeDtypeStruct((TOKENS, n), F32 if residual else BF16),
        scratch_shapes=[pltpu.VMEM((k, tn), BF16)],
        compiler_params=_cparams(2, vmem_mb),
        name="out_proj" if residual else "in_proj",
    )(*args)


def _mla_prep_kernel(cq_ref, ckv_ref, kr_ref, qn_ref, kvn_ref, wq_ref, wkv_ref, cos_ref, sin_ref,
                     q_ref, k_ref, v_ref):
    cos = cos_ref[...]
    sin = sin_ref[...]

    def rope(x):
        return x * cos + pltpu.roll(x, LANES // 2, 1) * sin

    def rms(x, g):
        xf = x.astype(F32)
        return (xf * lax.rsqrt(jnp.mean(xf * xf, axis=-1, keepdims=True) + EPS)) * g

    cq = rms(cq_ref[...], qn_ref[...]).astype(BF16)
    q = jnp.dot(cq, wq_ref[...].astype(BF16), preferred_element_type=F32)
    ckv = rms(ckv_ref[...], kvn_ref[...]).astype(BF16)
    kv = jnp.dot(ckv, wkv_ref[...].astype(BF16), preferred_element_type=F32)
    kr = rope(kr_ref[...].astype(F32)).astype(BF16)
    scale = (MLA_NOPE + MLA_ROPE) ** -0.5
    for h in range(MLA_HEADS):
        c0 = h * MLA_QK_PAD
        q_ref[:, c0:c0 + MLA_NOPE] = (q[:, c0:c0 + MLA_NOPE] * scale).astype(BF16)
        q_ref[:, c0 + MLA_NOPE:c0 + MLA_QK_PAD] = (
            rope(q[:, c0 + MLA_NOPE:c0 + MLA_QK_PAD]) * scale).astype(BF16)
        k_ref[:, c0:c0 + MLA_NOPE] = kv[:, h * MLA_NOPE:(h + 1) * MLA_NOPE].astype(BF16)
        k_ref[:, c0 + MLA_NOPE:c0 + MLA_QK_PAD] = kr
    v_ref[...] = kv[:, MLA_HEADS * MLA_NOPE:].astype(BF16)


def _mla_prep(proj, q_norm, kv_norm, wq, wkv, cos_m, sin_m):
    tm = 512
    c_q0 = 3 * SB_WIDTH
    c_kv0 = c_q0 + MLA_Q_LORA
    c_kr0 = c_kv0 + MLA_KV_LORA
    full = lambda shape: pl.BlockSpec(shape, lambda i: (0, 0))
    row = lambda w: pl.BlockSpec((tm, w), lambda i: (i, 0))
    qk_w = MLA_HEADS * MLA_QK_PAD
    return pl.pallas_call(
        _mla_prep_kernel,
        grid=(TOKENS // tm,),
        in_specs=[pl.BlockSpec((tm, MLA_Q_LORA), lambda i: (i, c_q0 // MLA_Q_LORA)),
                  pl.BlockSpec((tm, MLA_KV_LORA), lambda i: (i, c_kv0 // MLA_KV_LORA)),
                  pl.BlockSpec((tm, LANES), lambda i: (i, c_kr0 // LANES)),
                  full((1, MLA_Q_LORA)), full((1, MLA_KV_LORA)),
                  full((MLA_Q_LORA, qk_w)), full((MLA_KV_LORA, 2 * MLA_HEADS * MLA_NOPE)),
                  row(LANES), row(LANES)],
        out_specs=[row(qk_w), row(qk_w), row(MLA_HEADS * MLA_V)],
        out_shape=[jax.ShapeDtypeStruct((TOKENS, qk_w), BF16),
                   jax.ShapeDtypeStruct((TOKENS, qk_w), BF16),
                   jax.ShapeDtypeStruct((TOKENS, MLA_HEADS * MLA_V), BF16)],
        compiler_params=_cparams(1, 48),
        name="mla_prep",
    )(proj, proj, proj, q_norm, kv_norm, wq, wkv, cos_m, sin_m)


ATT_T = 256
ATT_HEADS = 2
LOG2E = math.log2(math.e)
SB_SKIP_LOG2 = -160.0


def _sb_attn_kernel(q_ref, k_ref, v_ref, o_ref, acc_ref, run_ref):
    t = ATT_T
    ii = lax.broadcasted_iota(I32, (t, t), 0)
    jj = lax.broadcasted_iota(I32, (t, t), 1)
    strict = jj < ii
    upper2 = (lax.broadcasted_iota(I32, (2 * t, t), 0) % t > lax.broadcasted_iota(I32, (2 * t, t), 1)
              ).astype(BF16)
    z_scale = HEAD_DIM ** -0.5 * LOG2E

    def block(q, kb, cols, run, diag):
        rows = pl.ds(pl.multiple_of(kb * t, t), t)
        z = lax.dot_general(q, k_ref[rows, cols], NT_DIMS, preferred_element_type=F32) * z_scale
        log_beta = jnp.minimum(z, 0.0) - jnp.log2(1.0 + jnp.exp2(jnp.minimum(z, -z)))
        log_keep = log_beta - z
        if diag:
            log_keep = jnp.where(strict, log_keep, 0.0)
        hi = log_keep.astype(BF16)
        lo = (log_keep - hi.astype(F32)).astype(BF16)
        within = jnp.dot(jnp.concatenate([hi, lo], axis=1), upper2, preferred_element_type=F32)
        a = jnp.exp2(log_beta + (within + run))
        if diag:
            a = jnp.where(strict, a, 0.0)
        pv = jnp.dot(a.astype(BF16), v_ref[rows, cols], preferred_element_type=F32)
        return pv, run + (within[:, 0:1] + log_keep[:, 0:1])

    nq = SEQ // t
    for qi in range(nq):
        qrows = pl.ds(qi * t, t)
        for hh in range(ATT_HEADS):
            cols = slice(hh * HEAD_DIM, (hh + 1) * HEAD_DIM)
            q = q_ref[qrows, cols]
            acc, run = block(q, qi, cols, jnp.zeros((t, 1), F32), True)
            if qi > 0:
                pv, run = block(q, qi - 1, cols, run, False)
                acc = acc + pv
            acc_ref[qrows, cols] = acc
            run_ref[qrows, cols] = jnp.broadcast_to(run, (t, HEAD_DIM))

    def more(qrows):
        return (jnp.max(run_ref[qrows, :]) > SB_SKIP_LOG2).astype(I32)

    for qi in range(2, nq):
        qrows = pl.ds(qi * t, t)

        def body(c, qrows=qrows):
            kb = c[0]
            for hh in range(ATT_HEADS):
                cols = slice(hh * HEAD_DIM, (hh + 1) * HEAD_DIM)
                pv, run = block(q_ref[qrows, cols], kb, cols, run_ref[qrows, hh * HEAD_DIM:hh * HEAD_DIM + 1],
                                False)
                acc_ref[qrows, cols] = acc_ref[qrows, cols] + pv
                run_ref[qrows, cols] = jnp.broadcast_to(run, (t, HEAD_DIM))
            return kb - 1, more(qrows)

        lax.while_loop(lambda c: (c[0] >= 0) & (c[1] == 1), body, (jnp.int32(qi - 2), more(qrows)))

    o_ref[...] = acc_ref[...].astype(o_ref.dtype)


def _sb_attention(proj):
    w = ATT_HEADS * HEAD_DIM
    blk = lambda off: pl.BlockSpec((SEQ, w), lambda b, h: (b, off + h))
    n = SB_HEADS // ATT_HEADS
    return pl.pallas_call(
        _sb_attn_kernel,
        grid=(BATCH, n),
        in_specs=[blk(0), blk(n), blk(2 * n)],
        out_specs=pl.BlockSpec((SEQ, w), lambda b, h: (b, h)),
        out_shape=jax.ShapeDtypeStruct((TOKENS, SB_WIDTH), BF16),
        scratch_shapes=[pltpu.VMEM((SEQ, w), F32), pltpu.VMEM((SEQ, w), F32)],
        compiler_params=_cparams(2, 48),
        name="sb_attention",
    )(proj, proj, proj)


def _mla_attn_kernel(q_ref, k_ref, v_ref, o_ref):
    t = ATT_T
    ii = lax.broadcasted_iota(I32, (t, t), 0)
    jj = lax.broadcasted_iota(I32, (t, t), 1)
    causal = jj <= ii
    neg = -1e30

    def step(q, kb, hh, carry, diag):
        m, l, acc = carry
        rows = pl.ds(pl.multiple_of(kb * t, t), t)
        s = lax.dot_general(q, k_ref[rows, hh * MLA_QK_PAD:(hh + 1) * MLA_QK_PAD], NT_DIMS,
                            preferred_element_type=F32)
        if diag:
            s = jnp.where(causal, s, neg)
        m_new = jnp.maximum(m, jnp.max(s, axis=1, keepdims=True))
        alpha = jnp.exp(m - m_new)
        p = jnp.exp(s - m_new)
        l = alpha * l + jnp.sum(p, axis=1, keepdims=True)
        acc = alpha * acc + jnp.dot(p.astype(BF16), v_ref[rows, hh * MLA_V:(hh + 1) * MLA_V],
                                    preferred_element_type=F32)
        return m_new, l, acc

    for qi in range(SEQ // t):
        qrows = pl.ds(qi * t, t)
        for hh in range(ATT_HEADS):
            q = q_ref[qrows, hh * MLA_QK_PAD:(hh + 1) * MLA_QK_PAD]
            c = (jnp.full((t, 1), neg, F32), jnp.zeros((t, 1), F32), jnp.zeros((t, MLA_V), F32))
            for kb in range(qi):
                c = step(q, kb, hh, c, False)
            _, l, acc = step(q, qi, hh, c, True)
            o_ref[qrows, hh * MLA_V:(hh + 1) * MLA_V] = (acc / l).astype(o_ref.dtype)


def _mla_attention(q_cat, k_cat, v):
    qk = pl.BlockSpec((SEQ, ATT_HEADS * MLA_QK_PAD), lambda b, h: (b, h))
    vo = pl.BlockSpec((SEQ, ATT_HEADS * MLA_V), lambda b, h: (b, h))
    return pl.pallas_call(
        _mla_attn_kernel,
        grid=(BATCH, MLA_HEADS // ATT_HEADS),
        in_specs=[qk, qk, vo],
        out_specs=vo,
        out_shape=jax.ShapeDtypeStruct((TOKENS, MLA_HEADS * MLA_V), BF16),
        compiler_params=_cparams(2, 48),
        name="mla_attention",
    )(q_cat, k_cat, v)


def _retention_kernel(lg_ref, q_ref, k_ref, v_ref, g_ref, cos_ref, sin_ref, o_ref, state_ref):
    lg = lg_ref[pl.program_id(1)]
    c = CHUNK
    half = RET_QK // 2
    ii = lax.broadcasted_iota(I32, (c, c), 0)
    jj = lax.broadcasted_iota(I32, (c, c), 1)
    rel = (ii - jj).astype(F32)
    intra = jnp.where(rel >= 0, jnp.exp(jnp.maximum(rel, 0.0) * lg), 0.0)
    idx = lax.broadcasted_iota(I32, (c, 1), 0).astype(F32)
    q_decay = jnp.exp((idx + 1.0) * lg)
    k_decay = jnp.exp((c - 1.0 - idx) * lg)
    chunk_decay = jnp.exp(jnp.full((1, 1), float(c), F32) * lg)
    state_ref[...] = jnp.zeros_like(state_ref)

    def rope(x, cos, sin):
        x1, x2 = x[:, :half], x[:, half:]
        return jnp.concatenate([x1 * cos - x2 * sin, x1 * sin + x2 * cos], axis=1)

    for ci in range(SEQ // c):
        rows = pl.ds(ci * c, c)
        cos = cos_ref[rows, :]
        sin = sin_ref[rows, :]
        qr = rope(q_ref[rows, :].astype(F32), cos, sin)
        kr = rope(k_ref[rows, :].astype(F32), cos, sin) * (RET_QK ** -0.5)
        v = v_ref[rows, :]
        qb = qr.astype(BF16)
        scores = lax.dot_general(qb, kr.astype(BF16), NT_DIMS, preferred_element_type=F32) * intra
        inner = jnp.dot(scores.astype(BF16), v, preferred_element_type=F32)
        st = state_ref[...]
        cross = jnp.dot(qb, st.astype(BF16), preferred_element_type=F32) * q_decay
        kd_t = jnp.transpose(kr * k_decay).astype(BF16)
        state_ref[...] = st * chunk_decay + jnp.dot(kd_t, v, preferred_element_type=F32)
        o = inner + cross
        o = o * lax.rsqrt(jnp.mean(o * o, axis=-1, keepdims=True) + EPS)
        gg = g_ref[rows, :].astype(F32)
        o_ref[rows, :] = ((gg * jax.nn.sigmoid(gg)) * o).astype(o_ref.dtype)


def _retention(proj, cos_r, sin_r):
    log_gamma = jnp.log1p(-jnp.exp2(-5.0 - jnp.arange(RET_HEADS, dtype=F32)))
    k0 = RET_QK_WIDTH // RET_QK
    v0 = 2 * RET_QK_WIDTH // RET_V
    g0 = (2 * RET_QK_WIDTH + RET_V_WIDTH) // RET_V
    tab = pl.BlockSpec((SEQ, LANES), lambda b, h: (b, 0))
    return pl.pallas_call(
        _retention_kernel,
        grid=(BATCH, RET_HEADS),
        in_specs=[pl.BlockSpec(memory_space=pltpu.SMEM),
                  pl.BlockSpec((SEQ, RET_QK), lambda b, h: (b, h)),
                  pl.BlockSpec((SEQ, RET_QK), lambda b, h: (b, k0 + h)),
                  pl.BlockSpec((SEQ, RET_V), lambda b, h: (b, v0 + h)),
                  pl.BlockSpec((SEQ, RET_V), lambda b, h: (b, g0 + h)),
                  tab, tab],
        out_specs=pl.BlockSpec((SEQ, RET_V), lambda b, h: (b, h)),
        out_shape=jax.ShapeDtypeStruct((TOKENS, RET_V_WIDTH), BF16),
        scratch_shapes=[pltpu.VMEM((RET_QK, RET_V), F32)],
        compiler_params=_cparams(2, 48),
        name="retention",
    )(log_gamma, proj, proj, proj, proj, cos_r, sin_r)


ROUTER_TM = 512


def _router_kernel(x_ref, g_ref, sc_ref, sh_ref, w_ref, b_ref, ei_ref, wcol_ref, cnt_ref, hp_ref,
                   carry_ref):
    tm = ROUTER_TM

    @pl.when(pl.program_id(0) == 0)
    def _():
        carry_ref[...] = jnp.zeros_like(carry_ref)

    def split(a):
        hi = a.astype(BF16)
        return hi, (a - hi.astype(F32)).astype(BF16)

    h = _norm_mod(x_ref[...], g_ref[...], sc_ref[0], sh_ref[0])
    _to_row_tiles(hp_ref, h, tm)
    h_hi, h_lo = split(h)
    w_hi, w_lo = split(w_ref[...])
    nt = lambda a, b: lax.dot_general(a, b, NT_DIMS, preferred_element_type=F32)
    logits = (nt(w_hi, h_hi) + nt(w_hi, h_lo) + nt(w_lo, h_hi)) + b_ref[...]
    e_log = logits[0:N_EXPERTS]
    g_log = logits[N_EXPERTS:N_EXPERTS + N_GROUPS]

    def top1(vals, n):
        rows = lax.broadcasted_iota(I32, (n, tm), 0)
        m = jnp.max(vals, axis=0, keepdims=True)
        return m, jnp.min(jnp.where(vals == m, rows, n), axis=0, keepdims=True), rows

    g_max, g_idx, _ = top1(g_log, N_GROUPS)
    g_w = 1.0 / jnp.sum(jnp.exp(g_log - g_max), axis=0, keepdims=True)
    sel = e_log[0:EXPERTS_PER_GROUP]
    for gi in range(1, N_GROUPS):
        sel = jnp.where(g_idx == gi, e_log[gi * EXPERTS_PER_GROUP:(gi + 1) * EXPERTS_PER_GROUP], sel)
    m1, i1, rows8 = top1(sel, EXPERTS_PER_GROUP)
    m2, i2, _ = top1(jnp.where(rows8 == i1, -jnp.inf, sel), EXPERTS_PER_GROUP)
    ratio = jnp.exp(m2 - m1)
    w1 = g_w / (1.0 + ratio)
    w2 = (g_w * ratio) / (1.0 + ratio)
    e1 = g_idx * EXPERTS_PER_GROUP + i1
    e2 = g_idx * EXPERTS_PER_GROUP + i2

    rows32 = lax.broadcasted_iota(I32, (N_EXPERTS, tm), 0)
    hit1 = rows32 == e1
    hit2 = rows32 == e2
    onehot = jnp.where(hit1 | hit2, 1.0, 0.0)
    jj = lax.broadcasted_iota(I32, (tm, tm), 0)
    tt = lax.broadcasted_iota(I32, (tm, tm), 1)
    before = (jj < tt).astype(BF16)
    rank_e = jnp.dot(onehot.astype(BF16), before, preferred_element_type=F32) + carry_ref[:, 0:1]
    r1 = jnp.sum(jnp.where(hit1, rank_e, 0.0), axis=0, keepdims=True)
    r2 = jnp.sum(jnp.where(hit2, rank_e, 0.0), axis=0, keepdims=True)
    carry_ref[...] = carry_ref[...] + jnp.sum(onehot, axis=1, keepdims=True)
    cnt_ref[...] = carry_ref[...]

    ei_ref[...] = jnp.where(rows8 == 0, e1, jnp.where(rows8 == 1, e2, jnp.where(
        rows8 == 2, r1.astype(I32), jnp.where(rows8 == 3, r2.astype(I32), 0))))
    rows128 = lax.broadcasted_iota(I32, (LANES, tm), 0)
    wrows = jnp.where(rows128 == 0, w1, jnp.where(rows128 == 1, w2, 0.0))
    wcol_ref[...] = jnp.transpose(wrows)


def _router(x2d, g, scale, shift, w_t, b_rows):
    tm = ROUTER_TM
    per_batch = SEQ // tm
    mod = pl.BlockSpec((1, 1, D_MODEL), lambda i: (i // per_batch, 0, 0))
    return pl.pallas_call(
        _router_kernel,
        grid=(TOKENS // tm,),
        in_specs=[pl.BlockSpec((tm, D_MODEL), lambda i: (i, 0)),
                  pl.BlockSpec((1, D_MODEL), lambda i: (0, 0)),
                  mod, mod,
                  pl.BlockSpec((ROUTER_ROWS, D_MODEL), lambda i: (0, 0)),
                  pl.BlockSpec((ROUTER_ROWS, tm), lambda i: (0, 0))],
        out_specs=[pl.BlockSpec((SUBLANES, tm), lambda i: (0, i)),
                   pl.BlockSpec((tm, LANES), lambda i: (i, 0)),
                   pl.BlockSpec((N_EXPERTS, LANES), lambda i: (0, 0)),
                   pl.BlockSpec((tm * SUBLANES, LANES), lambda i: (i, 0))],
        out_shape=[jax.ShapeDtypeStruct((SUBLANES, TOKENS), I32),
                   jax.ShapeDtypeStruct((TOKENS, LANES), F32),
                   jax.ShapeDtypeStruct((N_EXPERTS, LANES), F32),
                   jax.ShapeDtypeStruct((TOKENS * SUBLANES, LANES), U32)],
        scratch_shapes=[pltpu.VMEM((N_EXPERTS, LANES), F32)],
        compiler_params=_cparams(1, 48),
        name="moe_router",
    )(x2d, g, scale, shift, w_t, b_rows)


TOKEN_BITS = 13
assert TOKENS == 1 << TOKEN_BITS
MAP_TILES = MOE_TILES + 2
OUT_ROWS = 2 * TOKENS + 2 * MOE_TM


def _row_copy(src, dst, sem):
    return pltpu.make_async_copy(src, dst, sem)


def _experts_kernel(te_ref, tf_ref, tv_ref, pv_ref, nx_ref, ws_ref, map_ref,
                    hp_ref, wg_hbm, wu_hbm, wd_hbm, y2_ref,
                    wgf, wuf, wdf, wgb, wub, wdb, xbuf, ybuf, wsem, gsem, ssem, *, layer):
    tm = MOE_TM
    j = pl.program_id(0)
    cur = j % 2

    def weight_copies(e, s):
        return [_row_copy(wg_hbm.at[layer, e], wgf.at[s], wsem.at[s]),
                _row_copy(wu_hbm.at[layer, e], wuf.at[s], wsem.at[s]),
                _row_copy(wd_hbm.at[layer, e], wdf.at[s], wsem.at[s])]

    def for_rows(fn, unrolled):
        if unrolled:
            for r in range(tm):
                fn(r)
        else:
            lax.fori_loop(0, tm, lambda r, c: (fn(r), c)[1], 0)

    def start_gather(tile, slot, unrolled):
        def one(r):
            tok = map_ref[tile * tm + r] & (TOKENS - 1)
            src = hp_ref.at[pl.ds(pl.multiple_of(tok * SUBLANES, SUBLANES), SUBLANES), :]
            _row_copy(src, xbuf.at[slot, pl.ds(pl.multiple_of(r * SUBLANES, SUBLANES), SUBLANES), :],
                      gsem.at[slot]).start()
        for_rows(one, unrolled)

    def start_scatter(tile, slot, unrolled):
        def one(r):
            dst = map_ref[tile * tm + r] >> TOKEN_BITS
            _row_copy(ybuf.at[slot, pl.ds(pl.multiple_of(r * SUBLANES, SUBLANES), SUBLANES), :],
                      y2_ref.at[pl.ds(pl.multiple_of(dst * SUBLANES, SUBLANES), SUBLANES), :],
                      ssem.at[slot]).start()
        for_rows(one, unrolled)

    def wait_gather(slot):
        _row_copy(hp_ref.at[pl.ds(0, tm * SUBLANES), :], xbuf.at[slot], gsem.at[slot]).wait()

    def wait_scatter(slot):
        _row_copy(ybuf.at[slot], y2_ref.at[pl.ds(0, tm * SUBLANES), :], ssem.at[slot]).wait()

    @pl.when(j == 0)
    def _():
        ybuf[...] = jnp.zeros_like(ybuf)
        for cp in weight_copies(te_ref[0], 0):
            cp.start()
        start_gather(0, 0, False)
        start_scatter(MOE_TILES + 1, 0, False)

    @pl.when(pv_ref[j] == 1)
    def _():
        wait_gather(cur)

    @pl.when((tv_ref[j] == 1) & (tf_ref[j] == 1))
    def _():
        s = ws_ref[j]

        @pl.when(nx_ref[j] >= 0)
        def _():
            for cp in weight_copies(nx_ref[j], 1 - s):
                cp.start()

        for cp in weight_copies(te_ref[j], s):
            cp.wait()
        wgb[...] = wgf[s].astype(BF16)
        wub[...] = wuf[s].astype(BF16)
        wdb[...] = wdf[s].astype(BF16)

    @pl.when(tv_ref[j] == 1)
    def _():
        start_gather(jnp.minimum(j + 1, MOE_TILES - 1), 1 - cur, True)
        start_scatter(jnp.where(j == 0, MOE_TILES, j - 1), 1 - cur, True)
        x = _from_row_tiles(xbuf.at[cur], tm).astype(BF16)
        gate = jnp.dot(x, wgb[...], preferred_element_type=F32)
        up = jnp.dot(x, wub[...], preferred_element_type=F32)
        a = ((gate * jax.nn.sigmoid(gate)) * up).astype(BF16)
        y = jnp.dot(a, wdb[...], preferred_element_type=F32)
        wait_scatter(cur)
        _to_row_tiles(ybuf.at[cur], y, tm)

    @pl.when((tv_ref[j] == 0) & (pv_ref[j] == 1))
    def _():
        start_scatter(j - 1, 1 - cur, False)
        wait_scatter(cur)
        wait_scatter(1 - cur)


def _experts(plan, row_map, h_rows, w_gate, w_up, w_down, layer):
    tm = MOE_TM
    any_spec = pl.BlockSpec(memory_space=pl.ANY)
    grid_spec = pltpu.PrefetchScalarGridSpec(
        num_scalar_prefetch=7,
        grid=(MOE_TILES,),
        in_specs=[any_spec, any_spec, any_spec, any_spec],
        out_specs=any_spec,
        scratch_shapes=[pltpu.VMEM((2, D_MODEL, EXPERT_HIDDEN), F32),
                        pltpu.VMEM((2, D_MODEL, EXPERT_HIDDEN), F32),
                        pltpu.VMEM((2, EXPERT_HIDDEN, D_MODEL), F32),
                        pltpu.VMEM((D_MODEL, EXPERT_HIDDEN), BF16),
                        pltpu.VMEM((D_MODEL, EXPERT_HIDDEN), BF16),
                        pltpu.VMEM((EXPERT_HIDDEN, D_MODEL), BF16),
                        pltpu.VMEM((2, tm * SUBLANES, LANES), U32),
                        pltpu.VMEM((2, tm * SUBLANES, LANES), U32),
                        pltpu.SemaphoreType.DMA((2,)),
                        pltpu.SemaphoreType.DMA((2,)),
                        pltpu.SemaphoreType.DMA((2,))])
    return pl.pallas_call(
        functools.partial(_experts_kernel, layer=layer),
        grid_spec=grid_spec,
        out_shape=jax.ShapeDtypeStruct((OUT_ROWS * SUBLANES, LANES), U32),
        compiler_params=_cparams(1, 56),
        name="moe_experts",
    )(*plan, row_map, h_rows, w_gate, w_up, w_down)


COMBINE_TM = 256


def _combine_kernel(x_ref, gate_ref, wcol_ref, y0_ref, y1_ref, *refs, final):
    if final:
        fg_ref, o_ref = refs
    else:
        ng_ref, nsc_ref, nsh_ref, o_ref, h_ref = refs
    tm = COMBINE_TM
    y0 = _from_row_tiles(y0_ref, tm)
    y1 = _from_row_tiles(y1_ref, tm)
    out = x_ref[...] + gate_ref[0] * (wcol_ref[:, 0:1] * y0 + wcol_ref[:, 1:2] * y1)
    if final:
        o_ref[...] = (out * lax.rsqrt(jnp.mean(out * out, axis=-1, keepdims=True) + EPS)) * fg_ref[...]
    else:
        o_ref[...] = out
        h_ref[...] = _norm_mod(out, ng_ref[...], nsc_ref[0], nsh_ref[0]).astype(h_ref.dtype)


def _combine(x2d, gate, wcol, y2, final_g=None, next_norm=None):
    tm = COMBINE_TM
    per_batch = SEQ // tm
    final = final_g is not None
    row = pl.BlockSpec((tm, D_MODEL), lambda i: (i, 0))
    vec = pl.BlockSpec((1, D_MODEL), lambda i: (0, 0))
    mod = pl.BlockSpec((1, 1, D_MODEL), lambda i: (i // per_batch, 0, 0))
    slot0 = pl.BlockSpec((tm * SUBLANES, LANES), lambda i: (i, 0))
    slot1 = pl.BlockSpec((tm * SUBLANES, LANES), lambda i: (TOKENS // tm + i, 0))
    extra_specs, extra = ([vec], [final_g]) if final else ([vec, mod, mod], list(next_norm))
    out_f32 = jax.ShapeDtypeStruct((TOKENS, D_MODEL), F32)
    return pl.pallas_call(
        functools.partial(_combine_kernel, final=final),
        grid=(TOKENS // tm,),
        in_specs=[row, mod, pl.BlockSpec((tm, LANES), lambda i: (i, 0)), slot0, slot1] + extra_specs,
        out_specs=row if final else [row, row],
        out_shape=out_f32 if final else [out_f32, jax.ShapeDtypeStruct((TOKENS, D_MODEL), BF16)],
        compiler_params=_cparams(1, 48),
        name="moe_combine",
    )(x2d, gate, wcol, y2, y2, *extra)


def _moe_plan(counts):
    counts = counts.astype(I32)
    tiles = (counts + (MOE_TM - 1)) // MOE_TM
    cum = jnp.cumsum(tiles)
    first_tile = cum - tiles
    n_used = cum[-1]
    j = jnp.arange(MOE_TILES, dtype=I32)
    te = jnp.minimum(jnp.sum((j[:, None] >= cum[None, :]).astype(I32), axis=1), N_EXPERTS - 1)
    valid = j < n_used
    te = jnp.where(valid, te, jnp.take(te, n_used - 1))
    tf = (valid & (j == jnp.take(first_tile, te))).astype(I32)
    tv = valid.astype(I32)
    pv = jnp.concatenate([jnp.ones((1,), I32), tv[:-1]])
    has_rows = tiles > 0
    ordinal = jnp.cumsum(has_rows.astype(I32)) - 1
    e_ids = jnp.arange(N_EXPERTS, dtype=I32)
    later = (e_ids[None, :] > e_ids[:, None]) & has_rows[None, :]
    succ = jnp.min(jnp.where(later, e_ids[None, :], N_EXPERTS), axis=1)
    succ = jnp.where(succ == N_EXPERTS, -1, succ)
    nx = jnp.take(succ, te)
    ws = jnp.take(ordinal, te) % 2
    return (te, tf, tv, pv, nx, ws), first_tile * MOE_TM


def _row_map(pos0, pos1):
    p = jnp.arange(MAP_TILES * MOE_TM, dtype=I32)
    tile = p // MOE_TM
    odd = jnp.where(tile < MOE_TILES, tile % 2, jnp.where(tile == MOE_TILES, 1, 0))
    trash = 2 * TOKENS + odd * MOE_TM + p % MOE_TM
    tok = jnp.arange(TOKENS, dtype=I32)
    rm = trash << TOKEN_BITS
    rm = rm.at[pos0].set(tok | (tok << TOKEN_BITS), unique_indices=True)
    return rm.at[pos1].set(tok | ((TOKENS + tok) << TOKEN_BITS), unique_indices=True)


def _positions_kernel(ei_ref, off_ref, pos_ref):
    tm = ei_ref.shape[1]
    rows32 = lax.broadcasted_iota(I32, (N_EXPERTS, tm), 0)
    rows8 = lax.broadcasted_iota(I32, (SUBLANES, tm), 0)
    off = off_ref[:, 0:1]

    def pos(k):
        first = jnp.sum(jnp.where(rows32 == ei_ref[k:k + 1, :], off, 0), axis=0, keepdims=True)
        return first + ei_ref[k + 2:k + 3, :]

    pos_ref[...] = jnp.where(rows8 == 0, pos(0), jnp.where(rows8 == 1, pos(1), 0))


def _positions(ei, row_off):
    tm = 2048
    blk = pl.BlockSpec((SUBLANES, tm), lambda i: (0, i))
    return pl.pallas_call(
        _positions_kernel,
        grid=(TOKENS // tm,),
        in_specs=[blk, pl.BlockSpec((N_EXPERTS, LANES), lambda i: (0, 0))],
        out_specs=blk,
        out_shape=jax.ShapeDtypeStruct((SUBLANES, TOKENS), I32),
        compiler_params=_cparams(1, 32),
        name="moe_positions",
    )(ei, jnp.broadcast_to(row_off[:, None], (N_EXPERTS, LANES)))


def _moe_layer(x2d, layer, mods, norm_g, w_group, b_group, w_expert, b_expert, w_gate, w_up, w_down,
               final_g=None, next_norm=None):
    shift, scale, gate = mods
    w_t = jnp.concatenate([w_expert.T, w_group.T,
                           jnp.zeros((ROUTER_ROWS - N_EXPERTS - N_GROUPS, D_MODEL), F32)], axis=0)
    b_rows = jnp.concatenate([b_expert, b_group, jnp.zeros((ROUTER_ROWS - N_EXPERTS - N_GROUPS,), F32)])
    b_rows = jnp.broadcast_to(b_rows[:, None], (ROUTER_ROWS, ROUTER_TM))
    ei, wcol, cnt, h_rows = _router(x2d, norm_g, scale, shift, w_t, b_rows)
    plan, row_off = _moe_plan(cnt[:, 0])
    pos = _positions(ei, row_off)
    y2 = _experts(plan, _row_map(pos[0], pos[1]), h_rows, w_gate, w_up, w_down, layer)
    return _combine(x2d, gate, wcol, y2, final_g=final_g, next_norm=next_norm)


def _split_mods(mod, layer):
    m = mod[layer]
    return tuple(m[:, k * D_MODEL:(k + 1) * D_MODEL].reshape(BATCH, 1, D_MODEL) for k in range(3))


def kernel(x, c, positions, w_mod_mix, b_mod_mix, norm_mix, w_mod_ffn, b_mod_ffn, norm_ffn, ev_w_in, ev_q_norm, ev_w_q_up, ev_kv_norm, ev_w_kv_up, ev_w_out, od_w_in, od_w_out, moe_w_group, moe_b_group, moe_w_expert, moe_b_expert, moe_w_gate, moe_w_up, moe_w_down, final_norm):
    x2d = x.reshape(TOKENS, D_MODEL)
    c_lanes = jnp.broadcast_to(c[:, :, None], (BATCH, D_MODEL, LANES))
    mod_mix = _mods(c_lanes, w_mod_mix, b_mod_mix)
    mod_ffn = _mods(c_lanes, w_mod_ffn, b_mod_ffn)
    pos_lanes = jnp.broadcast_to(positions.reshape(TOKENS, 1), (TOKENS, LANES))
    cos_m, sin_m, cos_r, sin_r = _rope_tables(pos_lanes)
    final_g = final_norm.reshape(1, D_MODEL)

    shift, scale, gate = _split_mods(mod_mix, 0)
    half = MLA_ROPE // 2
    w_in = ev_w_in[0]
    c_kr0 = 3 * SB_WIDTH + MLA_Q_LORA + MLA_KV_LORA
    zc = lambda n: jnp.zeros((D_MODEL, n), F32)
    w_in_pad = jnp.concatenate([w_in[:, :c_kr0], w_in[:, c_kr0:c_kr0 + half], zc(half),
                                w_in[:, c_kr0 + half:], zc(half), zc(LANES)], axis=1).astype(BF16)
    h = _norm_mod_rows(x2d, norm_mix[0].reshape(1, D_MODEL), scale, shift)
    proj = _proj([h], w_in_pad, 1024, 48)
    wq = ev_w_q_up[0].reshape(MLA_Q_LORA, MLA_HEADS, MLA_NOPE + MLA_ROPE)
    zq = jnp.zeros((MLA_Q_LORA, MLA_HEADS, half), F32)
    wq = jnp.concatenate([wq[:, :, :MLA_NOPE], wq[:, :, MLA_NOPE:MLA_NOPE + half], zq,
                          wq[:, :, MLA_NOPE + half:], zq], axis=2).reshape(MLA_Q_LORA, MLA_HEADS * MLA_QK_PAD)
    wkv = ev_w_kv_up[0].reshape(MLA_KV_LORA, MLA_HEADS, MLA_NOPE + MLA_V)
    wkv = jnp.concatenate([wkv[:, :, :MLA_NOPE].reshape(MLA_KV_LORA, MLA_HEADS * MLA_NOPE),
                           wkv[:, :, MLA_NOPE:].reshape(MLA_KV_LORA, MLA_HEADS * MLA_V)], axis=1)
    q_cat, k_cat, v_mla = _mla_prep(proj, ev_q_norm[0].reshape(1, MLA_Q_LORA),
                                    ev_kv_norm[0].reshape(1, MLA_KV_LORA), wq, wkv, cos_m, sin_m)
    o_sb = _sb_attention(proj)
    o_mla = _mla_attention(q_cat, k_cat, v_mla)
    x2d = _proj([o_sb, o_mla], ev_w_out[0], 512, 40, x2d, gate)
    shift, scale, gate = _split_mods(mod_mix, 1)
    x2d, h = _moe_layer(x2d, 0, _split_mods(mod_ffn, 0), norm_ffn[0].reshape(1, D_MODEL),
                        moe_w_group[0], moe_b_group[0], moe_w_expert[0], moe_b_expert[0],
                        moe_w_gate, moe_w_up, moe_w_down,
                        next_norm=(norm_mix[1].reshape(1, D_MODEL), scale, shift))

    proj = _proj([h], od_w_in[0], 1024, 48)
    o_ret = _retention(proj, cos_r, sin_r)
    x2d = _proj([o_ret], od_w_out[0], 512, 56, x2d, gate)
    out = _moe_layer(x2d, 1, _split_mods(mod_ffn, 1), norm_ffn[1].reshape(1, D_MODEL),
                     moe_w_group[1], moe_b_group[1], moe_w_expert[1], moe_b_expert[1],
                     moe_w_gate, moe_w_up, moe_w_down, final_g=final_g)
    return out.reshape(BATCH, SEQ, D_MODEL)
```

```python
import functools
import math

import jax
import jax.numpy as jnp
from jax import lax
from jax.experimental import pallas as pl
from jax.experimental.pallas import tpu as pltpu

F32 = jnp.float32
BF16 = jnp.bfloat16
I32 = jnp.int32

D_MODEL = 2048
BATCH = 4
SEQ = 2048
DEPTH = 2
TOKENS = BATCH * SEQ

HEAD_DIM = 128
SB_HEADS = 8
SB_WIDTH = SB_HEADS * HEAD_DIM
MLA_HEADS = 8
MLA_Q_LORA = 512
MLA_KV_LORA = 256
MLA_NOPE = 128
MLA_ROPE = 64
MLA_V = 128
MLA_QK_PAD = 256
EVEN_IN_PAD = 4096
RET_HEADS = 8
RET_QK = 256
RET_V = 512
RET_QK_WIDTH = RET_HEADS * RET_QK
RET_V_WIDTH = RET_HEADS * RET_V
ODD_IN_WIDTH = 2 * RET_QK_WIDTH + 2 * RET_V_WIDTH
N_GROUPS = 4
EXPERTS_PER_GROUP = 8
N_EXPERTS = N_GROUPS * EXPERTS_PER_GROUP
EXPERT_HIDDEN = 512
CHUNK = 128
ROPE_BASE = 10000.0
EPS = 1e-6

LANES = 128
SUBLANES = 8
HALF_D = D_MODEL // 2
assert HALF_D == SUBLANES * LANES
U32 = jnp.uint32

ROUTER_ROWS = 48
MOE_TM = 256
MOE_TILES = (2 * TOKENS) // MOE_TM + N_EXPERTS
MOE_ROWS = MOE_TILES * MOE_TM
ZERO_ROWS = MOE_TM // 2

NT_DIMS = (((1,), (1,)), ((), ()))


def _cparams(n_grid, vmem_mb):
    return pltpu.CompilerParams(dimension_semantics=("arbitrary",) * n_grid,
                                vmem_limit_bytes=vmem_mb * 1024 * 1024)


def _norm_mod(x, g, scale, shift):
    y = x * lax.rsqrt(jnp.mean(x * x, axis=-1, keepdims=True) + EPS)
    return (y * g) * (1.0 + scale) + shift


def _to_row_tiles(ref_at, val, rows):
    for b in range(SUBLANES):
        lo = val[:, b * LANES:(b + 1) * LANES]
        hi = val[:, HALF_D + b * LANES:HALF_D + (b + 1) * LANES]
        ref_at[pl.ds(b, rows, stride=SUBLANES), :] = pltpu.pack_elementwise([lo, hi], packed_dtype=BF16)


def _from_row_tiles(ref_at, rows):
    lo, hi = [], []
    for b in range(SUBLANES):
        w = ref_at[pl.ds(b, rows, stride=SUBLANES), :]
        lo.append(pltpu.unpack_elementwise(w, index=0, packed_dtype=BF16, unpacked_dtype=F32))
        hi.append(pltpu.unpack_elementwise(w, index=1, packed_dtype=BF16, unpacked_dtype=F32))
    return jnp.concatenate(lo + hi, axis=1)


def _mods_kernel(c_ref, w_ref, b_ref, o_ref, cond_ref):
    @pl.when((pl.program_id(0) == 0) & (pl.program_id(1) == 0))
    def _():
        c = c_ref[...]
        cond_ref[...] = c * jax.nn.sigmoid(c)

    tn = w_ref.shape[2]
    rows = []
    for b in range(BATCH):
        cb = cond_ref[b]
        cols = [jnp.sum(w_ref[0, :, j * LANES:(j + 1) * LANES] * cb, axis=0, keepdims=True)
                for j in range(tn // LANES)]
        rows.append(jnp.concatenate(cols, axis=1))
    o_ref[0] = jnp.concatenate(rows, axis=0) + b_ref[0]


def _mods(c_lanes, w_mod, b_mod):
    tn = 512
    n3 = 3 * D_MODEL
    return pl.pallas_call(
        _mods_kernel,
        grid=(DEPTH, n3 // tn),
        in_specs=[pl.BlockSpec((BATCH, D_MODEL, LANES), lambda l, j: (0, 0, 0)),
                  pl.BlockSpec((1, D_MODEL, tn), lambda l, j: (l, 0, j)),
                  pl.BlockSpec((1, 1, tn), lambda l, j: (l, 0, j))],
        out_specs=pl.BlockSpec((1, BATCH, tn), lambda l, j: (l, 0, j)),
        out_shape=jax.ShapeDtypeStruct((DEPTH, BATCH, n3), F32),
        scratch_shapes=[pltpu.VMEM((BATCH, D_MODEL, LANES), F32)],
        compiler_params=_cparams(2, 32),
        name="adaln_mods",
    )(c_lanes, w_mod, b_mod.reshape(DEPTH, 1, n3))


def _rope_tables_kernel(pos_ref, fm_ref, sg_ref, fr_ref, cm_ref, sm_ref, cr_ref, sr_ref):
    p = pos_ref[...].astype(F32)
    am = p * fm_ref[...]
    cm_ref[...] = jnp.cos(am)
    sm_ref[...] = jnp.sin(am) * sg_ref[...]
    ar = p * fr_ref[...]
    cr_ref[...] = jnp.cos(ar)
    sr_ref[...] = jnp.sin(ar)


def _rope_tables(pos_lanes):
    half = MLA_ROPE // 2
    f_mla = jnp.exp(-math.log(ROPE_BASE) * jnp.arange(half, dtype=F32) / half)
    z = jnp.zeros((half,), F32)
    fm = jnp.concatenate([f_mla, z, f_mla, z]).reshape(1, LANES)
    sg = jnp.concatenate([-jnp.ones((half,), F32), z, jnp.ones((half,), F32), z]).reshape(1, LANES)
    hr = RET_QK // 2
    fr = jnp.exp(-math.log(ROPE_BASE) * jnp.arange(hr, dtype=F32) / hr).reshape(1, LANES)
    tm = 1024
    row = pl.BlockSpec((tm, LANES), lambda i: (i, 0))
    vec = pl.BlockSpec((1, LANES), lambda i: (0, 0))
    tab = jax.ShapeDtypeStruct((TOKENS, LANES), F32)
    return pl.pallas_call(
        _rope_tables_kernel,
        grid=(TOKENS // tm,),
        in_specs=[row, vec, vec, vec],
        out_specs=[row, row, row, row],
        out_shape=[tab, tab, tab, tab],
        compiler_params=_cparams(1, 32),
        name="rope_tables",
    )(pos_lanes, fm, sg, fr)


def _norm_mod_kernel(x_ref, g_ref, sc_ref, sh_ref, o_ref):
    o_ref[...] = _norm_mod(x_ref[...], g_ref[...], sc_ref[0], sh_ref[0]).astype(o_ref.dtype)


def _norm_mod_rows(x2d, g, scale, shift):
    tm = 512
    per_batch = SEQ // tm
    mod = pl.BlockSpec((1, 1, D_MODEL), lambda i: (i // per_batch, 0, 0))
    row = pl.BlockSpec((tm, D_MODEL), lambda i: (i, 0))
    return pl.pallas_call(
        _norm_mod_kernel,
        grid=(TOKENS // tm,),
        in_specs=[row, pl.BlockSpec((1, D_MODEL), lambda i: (0, 0)), mod, mod],
        out_specs=row,
        out_shape=jax.ShapeDtypeStruct((TOKENS, D_MODEL), BF16),
        compiler_params=_cparams(1, 32),
        name="norm_mod",
    )(x2d, g, scale, shift)


PROJ_TM = 1024


def _proj_kernel(*refs, n_in, residual):
    lhs_refs = refs[:n_in]
    if residual:
        w_ref, x_ref, gate_ref, o_ref, wb_ref = refs[n_in:]
    else:
        w_ref, o_ref, wb_ref = refs[n_in:]

    @pl.when(pl.program_id(1) == 0)
    def _():
        wb_ref[...] = w_ref[...].astype(BF16)

    acc = None
    off = 0
    for y_ref in lhs_refs:
        kk = y_ref.shape[1]
        d = jnp.dot(y_ref[...], wb_ref[off:off + kk, :], preferred_element_type=F32)
        acc = d if acc is None else acc + d
        off += kk
    if residual:
        o_ref[...] = x_ref[...] + gate_ref[0] * acc
    else:
        o_ref[...] = acc.astype(o_ref.dtype)


def _proj(ys, w, tn, vmem_mb, x2d=None, gate=None):
    tm = PROJ_TM
    k, n = w.shape
    per_batch = SEQ // tm
    residual = x2d is not None
    in_specs = [pl.BlockSpec((tm, y.shape[1]), lambda j, i: (i, 0)) for y in ys]
    in_specs.append(pl.BlockSpec((k, tn), lambda j, i: (0, j)))
    args = list(ys) + [w]
    if residual:
        in_specs += [pl.BlockSpec((tm, tn), lambda j, i: (i, j)),
                     pl.BlockSpec((1, 1, tn), lambda j, i: (i // per_batch, 0, j))]
        args += [x2d, gate]
    return pl.pallas_call(
        functools.partial(_proj_kernel, n_in=len(ys), residual=residual),
        grid=(n // tn, TOKENS // tm),
        in_specs=in_specs,
        out_specs=pl.BlockSpec((tm, tn), lambda j, i: (i, j)),
        out_shape=jax.ShapeDtypeStruct((TOKENS, n), F32 if residual else BF16),
        scratch_shapes=[pltpu.VMEM((k, tn), BF16)],
        compiler_params=_cparams(2, vmem_mb),
        name="out_proj" if residual else "in_proj",
    )(*args)


def _mla_prep_kernel(cq_ref, ckv_ref, kr_ref, qn_ref, kvn_ref, wq_ref, wkv_ref, cos_ref, sin_ref,
                     q_ref, k_ref, v_ref):
    cos = cos_ref[...]
    sin = sin_ref[...]

    def rope(x):
        return x * cos + pltpu.roll(x, LANES // 2, 1) * sin

    def rms(x, g):
        xf = x.astype(F32)
        return (xf * lax.rsqrt(jnp.mean(xf * xf, axis=-1, keepdims=True) + EPS)) * g

    cq = rms(cq_ref[...], qn_ref[...]).astype(BF16)
    q = jnp.dot(cq, wq_ref[...].astype(BF16), preferred_element_type=F32)
    ckv = rms(ckv_ref[...], kvn_ref[...]).astype(BF16)
    kv = jnp.dot(ckv, wkv_ref[...].astype(BF16), preferred_element_type=F32)
    kr = rope(kr_ref[...].astype(F32)).astype(BF16)
    scale = (MLA_NOPE + MLA_ROPE) ** -0.5
    for h in range(MLA_HEADS):
        c0 = h * MLA_QK_PAD
        q_ref[:, c0:c0 + MLA_NOPE] = (q[:, c0:c0 + MLA_NOPE] * scale).astype(BF16)
        q_ref[:, c0 + MLA_NOPE:c0 + MLA_QK_PAD] = (
            rope(q[:, c0 + MLA_NOPE:c0 + MLA_QK_PAD]) * scale).astype(BF16)
        k_ref[:, c0:c0 + MLA_NOPE] = kv[:, h * MLA_NOPE:(h + 1) * MLA_NOPE].astype(BF16)
        k_ref[:, c0 + MLA_NOPE:c0 + MLA_QK_PAD] = kr
    v_ref[...] = kv[:, MLA_HEADS * MLA_NOPE:].astype(BF16)


def _mla_prep(proj, q_norm, kv_norm, wq, wkv, cos_m, sin_m):
    tm = 512
    c_q0 = 3 * SB_WIDTH
    c_kv0 = c_q0 + MLA_Q_LORA
    c_kr0 = c_kv0 + MLA_KV_LORA
    full = lambda shape: pl.BlockSpec(shape, lambda i: (0, 0))
    row = lambda w: pl.BlockSpec((tm, w), lambda i: (i, 0))
    qk_w = MLA_HEADS * MLA_QK_PAD
    return pl.pallas_call(
        _mla_prep_kernel,
        grid=(TOKENS // tm,),
        in_specs=[pl.BlockSpec((tm, MLA_Q_LORA), lambda i: (i, c_q0 // MLA_Q_LORA)),
                  pl.BlockSpec((tm, MLA_KV_LORA), lambda i: (i, c_kv0 // MLA_KV_LORA)),
                  pl.BlockSpec((tm, LANES), lambda i: (i, c_kr0 // LANES)),
                  full((1, MLA_Q_LORA)), full((1, MLA_KV_LORA)),
                  full((MLA_Q_LORA, qk_w)), full((MLA_KV_LORA, 2 * MLA_HEADS * MLA_NOPE)),
                  row(LANES), row(LANES)],
        out_specs=[row(qk_w), row(qk_w), row(MLA_HEADS * MLA_V)],
        out_shape=[jax.ShapeDtypeStruct((TOKENS, qk_w), BF16),
                   jax.ShapeDtypeStruct((TOKENS, qk_w), BF16),
                   jax.ShapeDtypeStruct((TOKENS, MLA_HEADS * MLA_V), BF16)],
        compiler_params=_cparams(1, 48),
        name="mla_prep",
    )(proj, proj, proj, q_norm, kv_norm, wq, wkv, cos_m, sin_m)


ATT_T = 256
ATT_HEADS = 2
LOG2E = math.log2(math.e)
SB_SKIP_LOG2 = -160.0


def _sb_attn_kernel(q_ref, k_ref, v_ref, o_ref, acc_ref, run_ref):
    t = ATT_T
    ii = lax.broadcasted_iota(I32, (t, t), 0)
    jj = lax.broadcasted_iota(I32, (t, t), 1)
    strict = jj < ii
    upper2 = (lax.broadcasted_iota(I32, (2 * t, t), 0) % t > lax.broadcasted_iota(I32, (2 * t, t), 1)
              ).astype(BF16)
    z_scale = HEAD_DIM ** -0.5 * LOG2E

    def block(q, kb, cols, run, diag):
        rows = pl.ds(pl.multiple_of(kb * t, t), t)
        z = lax.dot_general(q, k_ref[rows, cols], NT_DIMS, preferred_element_type=F32) * z_scale
        log_beta = jnp.minimum(z, 0.0) - jnp.log2(1.0 + jnp.exp2(jnp.minimum(z, -z)))
        log_keep = log_beta - z
        if diag:
            log_keep = jnp.where(strict, log_keep, 0.0)
        hi = log_keep.astype(BF16)
        lo = (log_keep - hi.astype(F32)).astype(BF16)
        within = jnp.dot(jnp.concatenate([hi, lo], axis=1), upper2, preferred_element_type=F32)
        a = jnp.exp2(log_beta + (within + run))
        if diag:
            a = jnp.where(strict, a, 0.0)
        pv = jnp.dot(a.astype(BF16), v_ref[rows, cols], preferred_element_type=F32)
        return pv, run + (within[:, 0:1] + log_keep[:, 0:1])

    nq = SEQ // t
    for qi in range(nq):
        qrows = pl.ds(qi * t, t)
        for hh in range(ATT_HEADS):
            cols = slice(hh * HEAD_DIM, (hh + 1) * HEAD_DIM)
            q = q_ref[qrows, cols]
            acc, run = block(q, qi, cols, jnp.zeros((t, 1), F32), True)
            if qi > 0:
                pv, run = block(q, qi - 1, cols, run, False)
                acc = acc + pv
            acc_ref[qrows, cols] = acc
            run_ref[qrows, cols] = jnp.broadcast_to(run, (t, HEAD_DIM))

    def more(qrows):
        return (jnp.max(run_ref[qrows, :]) > SB_SKIP_LOG2).astype(I32)

    for qi in range(2, nq):
        qrows = pl.ds(qi * t, t)

        def body(c, qrows=qrows):
            kb = c[0]
            for hh in range(ATT_HEADS):
                cols = slice(hh * HEAD_DIM, (hh + 1) * HEAD_DIM)
                pv, run = block(q_ref[qrows, cols], kb, cols, run_ref[qrows, hh * HEAD_DIM:hh * HEAD_DIM + 1],
                                False)
                acc_ref[qrows, cols] = acc_ref[qrows, cols] + pv
                run_ref[qrows, cols] = jnp.broadcast_to(run, (t, HEAD_DIM))
            return kb - 1, more(qrows)

        lax.while_loop(lambda c: (c[0] >= 0) & (c[1] == 1), body, (jnp.int32(qi - 2), more(qrows)))

    o_ref[...] = acc_ref[...].astype(o_ref.dtype)


def _sb_attention(proj):
    w = ATT_HEADS * HEAD_DIM
    blk = lambda off: pl.BlockSpec((SEQ, w), lambda b, h: (b, off + h))
    n = SB_HEADS // ATT_HEADS
    return pl.pallas_call(
        _sb_attn_kernel,
        grid=(BATCH, n),
        in_specs=[blk(0), blk(n), blk(2 * n)],
        out_specs=pl.BlockSpec((SEQ, w), lambda b, h: (b, h)),
        out_shape=jax.ShapeDtypeStruct((TOKENS, SB_WIDTH), BF16),
        scratch_shapes=[pltpu.VMEM((SEQ, w), F32), pltpu.VMEM((SEQ, w), F32)],
        compiler_params=_cparams(2, 48),
        name="sb_attention",
    )(proj, proj, proj)


def _mla_attn_kernel(q_ref, k_ref, v_ref, o_ref):
    t = ATT_T
    ii = lax.broadcasted_iota(I32, (t, t), 0)
    jj = lax.broadcasted_iota(I32, (t, t), 1)
    causal = jj <= ii
    neg = -1e30

    def step(q, kb, hh, carry, diag):
        m, l, acc = carry
        rows = pl.ds(pl.multiple_of(kb * t, t), t)
        s = lax.dot_general(q, k_ref[rows, hh * MLA_QK_PAD:(hh + 1) * MLA_QK_PAD], NT_DIMS,
                            preferred_element_type=F32)
        if diag:
            s = jnp.where(causal, s, neg)
        m_new = jnp.maximum(m, jnp.max(s, axis=1, keepdims=True))
        alpha = jnp.exp(m - m_new)
        p = jnp.exp(s - m_new)
        l = alpha * l + jnp.sum(p, axis=1, keepdims=True)
        acc = alpha * acc + jnp.dot(p.astype(BF16), v_ref[rows, hh * MLA_V:(hh + 1) * MLA_V],
                                    preferred_element_type=F32)
        return m_new, l, acc

    for qi in range(SEQ // t):
        qrows = pl.ds(qi * t, t)
        for hh in range(ATT_HEADS):
            q = q_ref[qrows, hh * MLA_QK_PAD:(hh + 1) * MLA_QK_PAD]
            c = (jnp.full((t, 1), neg, F32), jnp.zeros((t, 1), F32), jnp.zeros((t, MLA_V), F32))
            for kb in range(qi):
                c = step(q, kb, hh, c, False)
            _, l, acc = step(q, qi, hh, c, True)
            o_ref[qrows, hh * MLA_V:(hh + 1) * MLA_V] = (acc / l).astype(o_ref.dtype)


def _mla_attention(q_cat, k_cat, v):
    qk = pl.BlockSpec((SEQ, ATT_HEADS * MLA_QK_PAD), lambda b, h: (b, h))
    vo = pl.BlockSpec((SEQ, ATT_HEADS * MLA_V), lambda b, h: (b, h))
    return pl.pallas_call(
        _mla_attn_kernel,
        grid=(BATCH, MLA_HEADS // ATT_HEADS),
        in_specs=[qk, qk, vo],
        out_specs=vo,
        out_shape=jax.ShapeDtypeStruct((TOKENS, MLA_HEADS * MLA_V), BF16),
        compiler_params=_cparams(2, 48),
        name="mla_attention",
    )(q_cat, k_cat, v)


def _retention_kernel(lg_ref, q_ref, k_ref, v_ref, g_ref, cos_ref, sin_ref, o_ref, state_ref):
    lg = lg_ref[pl.program_id(1)]
    c = CHUNK
    half = RET_QK // 2
    ii = lax.broadcasted_iota(I32, (c, c), 0)
    jj = lax.broadcasted_iota(I32, (c, c), 1)
    rel = (ii - jj).astype(F32)
    intra = jnp.where(rel >= 0, jnp.exp(jnp.maximum(rel, 0.0) * lg), 0.0)
    idx = lax.broadcasted_iota(I32, (c, 1), 0).astype(F32)
    q_decay = jnp.exp((idx + 1.0) * lg)
    k_decay = jnp.exp((c - 1.0 - idx) * lg)
    chunk_decay = jnp.exp(jnp.full((1, 1), float(c), F32) * lg)
    state_ref[...] = jnp.zeros_like(state_ref)

    def rope(x, cos, sin):
        x1, x2 = x[:, :half], x[:, half:]
        return jnp.concatenate([x1 * cos - x2 * sin, x1 * sin + x2 * cos], axis=1)

    for ci in range(SEQ // c):
        rows = pl.ds(ci * c, c)
        cos = cos_ref[rows, :]
        sin = sin_ref[rows, :]
        qr = rope(q_ref[rows, :].astype(F32), cos, sin)
        kr = rope(k_ref[rows, :].astype(F32), cos, sin) * (RET_QK ** -0.5)
        v = v_ref[rows, :]
        qb = qr.astype(BF16)
        scores = lax.dot_general(qb, kr.astype(BF16), NT_DIMS, preferred_element_type=F32) * intra
        inner = jnp.dot(scores.astype(BF16), v, preferred_element_type=F32)
        st = state_ref[...]
        cross = jnp.dot(qb, st.astype(BF16), preferred_element_type=F32) * q_decay
        kd_t = jnp.transpose(kr * k_decay).astype(BF16)
        state_ref[...] = st * chunk_decay + jnp.dot(kd_t, v, preferred_element_type=F32)
        o = inner + cross
        o = o * lax.rsqrt(jnp.mean(o * o, axis=-1, keepdims=True) + EPS)
        gg = g_ref[rows, :].astype(F32)
        o_ref[rows, :] = ((gg * jax.nn.sigmoid(gg)) * o).astype(o_ref.dtype)


def _retention(proj, cos_r, sin_r):
    log_gamma = jnp.log1p(-jnp.exp2(-5.0 - jnp.arange(RET_HEADS, dtype=F32)))
    k0 = RET_QK_WIDTH // RET_QK
    v0 = 2 * RET_QK_WIDTH // RET_V
    g0 = (2 * RET_QK_WIDTH + RET_V_WIDTH) // RET_V
    tab = pl.BlockSpec((SEQ, LANES), lambda b, h: (b, 0))
    return pl.pallas_call(
        _retention_kernel,
        grid=(BATCH, RET_HEADS),
        in_specs=[pl.BlockSpec(memory_space=pltpu.SMEM),
                  pl.BlockSpec((SEQ, RET_QK), lambda b, h: (b, h)),
                  pl.BlockSpec((SEQ, RET_QK), lambda b, h: (b, k0 + h)),
                  pl.BlockSpec((SEQ, RET_V), lambda b, h: (b, v0 + h)),
                  pl.BlockSpec((SEQ, RET_V), lambda b, h: (b, g0 + h)),
                  tab, tab],
        out_specs=pl.BlockSpec((SEQ, RET_V), lambda b, h: (b, h)),
        out_shape=jax.ShapeDtypeStruct((TOKENS, RET_V_WIDTH), BF16),
        scratch_shapes=[pltpu.VMEM((RET_QK, RET_V), F32)],
        compiler_params=_cparams(2, 48),
        name="retention",
    )(log_gamma, proj, proj, proj, proj, cos_r, sin_r)


ROUTER_TM = 512


def _router_kernel(x_ref, g_ref, sc_ref, sh_ref, w_ref, b_ref, ei_ref, wcol_ref, cnt_ref, hp_ref,
                   carry_ref):
    tm = ROUTER_TM

    @pl.when(pl.program_id(0) == 0)
    def _():
        carry_ref[...] = jnp.zeros_like(carry_ref)

    def split(a):
        hi = a.astype(BF16)
        return hi, (a - hi.astype(F32)).astype(BF16)

    h = _norm_mod(x_ref[...], g_ref[...], sc_ref[0], sh_ref[0])
    _to_row_tiles(hp_ref, h, tm)
    h_hi, h_lo = split(h)
    w_hi, w_lo = split(w_ref[...])
    nt = lambda a, b: lax.dot_general(a, b, NT_DIMS, preferred_element_type=F32)
    logits = (nt(w_hi, h_hi) + nt(w_hi, h_lo) + nt(w_lo, h_hi)) + b_ref[...]
    e_log = logits[0:N_EXPERTS]
    g_log = logits[N_EXPERTS:N_EXPERTS + N_GROUPS]

    def top1(vals, n):
        rows = lax.broadcasted_iota(I32, (n, tm), 0)
        m = jnp.max(vals, axis=0, keepdims=True)
        return m, jnp.min(jnp.where(vals == m, rows, n), axis=0, keepdims=True), rows

    g_max, g_idx, _ = top1(g_log, N_GROUPS)
    g_w = 1.0 / jnp.sum(jnp.exp(g_log - g_max), axis=0, keepdims=True)
    sel = e_log[0:EXPERTS_PER_GROUP]
    for gi in range(1, N_GROUPS):
        sel = jnp.where(g_idx == gi, e_log[gi * EXPERTS_PER_GROUP:(gi + 1) * EXPERTS_PER_GROUP], sel)
    m1, i1, rows8 = top1(sel, EXPERTS_PER_GROUP)
    m2, i2, _ = top1(jnp.where(rows8 == i1, -jnp.inf, sel), EXPERTS_PER_GROUP)
    ratio = jnp.exp(m2 - m1)
    w1 = g_w / (1.0 + ratio)
    w2 = (g_w * ratio) / (1.0 + ratio)
    e1 = g_idx * EXPERTS_PER_GROUP + i1
    e2 = g_idx * EXPERTS_PER_GROUP + i2

    rows32 = lax.broadcasted_iota(I32, (N_EXPERTS, tm), 0)
    hit1 = rows32 == e1
    hit2 = rows32 == e2
    onehot = jnp.where(hit1 | hit2, 1.0, 0.0)
    jj = lax.broadcasted_iota(I32, (tm, tm), 0)
    tt = lax.broadcasted_iota(I32, (tm, tm), 1)
    before = (jj < tt).astype(BF16)
    rank_e = jnp.dot(onehot.astype(BF16), before, preferred_element_type=F32) + carry_ref[:, 0:1]
    r1 = jnp.sum(jnp.where(hit1, rank_e, 0.0), axis=0, keepdims=True)
    r2 = jnp.sum(jnp.where(hit2, rank_e, 0.0), axis=0, keepdims=True)
    carry_ref[...] = carry_ref[...] + jnp.sum(onehot, axis=1, keepdims=True)
    cnt_ref[...] = carry_ref[...]

    ei_ref[...] = jnp.where(rows8 == 0, e1, jnp.where(rows8 == 1, e2, jnp.where(
        rows8 == 2, r1.astype(I32), jnp.where(rows8 == 3, r2.astype(I32), 0))))
    rows128 = lax.broadcasted_iota(I32, (LANES, tm), 0)
    wrows = jnp.where(rows128 == 0, w1, jnp.where(rows128 == 1, w2, 0.0))
    wcol_ref[...] = jnp.transpose(wrows)


def _router(x2d, g, scale, shift, w_t, b_rows):
    tm = ROUTER_TM
    per_batch = SEQ // tm
    mod = pl.BlockSpec((1, 1, D_MODEL), lambda i: (i // per_batch, 0, 0))
    return pl.pallas_call(
        _router_kernel,
        grid=(TOKENS // tm,),
        in_specs=[pl.BlockSpec((tm, D_MODEL), lambda i: (i, 0)),
                  pl.BlockSpec((1, D_MODEL), lambda i: (0, 0)),
                  mod, mod,
                  pl.BlockSpec((ROUTER_ROWS, D_MODEL), lambda i: (0, 0)),
                  pl.BlockSpec((ROUTER_ROWS, tm), lambda i: (0, 0))],
        out_specs=[pl.BlockSpec((SUBLANES, tm), lambda i: (0, i)),
                   pl.BlockSpec((tm, LANES), lambda i: (i, 0)),
                   pl.BlockSpec((N_EXPERTS, LANES), lambda i: (0, 0)),
                   pl.BlockSpec((tm * SUBLANES, LANES), lambda i: (i, 0))],
        out_shape=[jax.ShapeDtypeStruct((SUBLANES, TOKENS), I32),
                   jax.ShapeDtypeStruct((TOKENS, LANES), F32),
                   jax.ShapeDtypeStruct((N_EXPERTS, LANES), F32),
                   jax.ShapeDtypeStruct((TOKENS * SUBLANES, LANES), U32)],
        scratch_shapes=[pltpu.VMEM((N_EXPERTS, LANES), F32)],
        compiler_params=_cparams(1, 48),
        name="moe_router",
    )(x2d, g, scale, shift, w_t, b_rows)


TOKEN_BITS = 13
assert TOKENS == 1 << TOKEN_BITS
MAP_TILES = MOE_TILES + 2
WEIGHT_DMA_PRIORITY = 1
OUT_ROWS = 2 * TOKENS + 2 * MOE_TM


def _row_copy(src, dst, sem):
    return pltpu.make_async_copy(src, dst, sem)


def _experts_kernel(te_ref, tf_ref, tv_ref, pv_ref, nx_ref, ws_ref, map_ref,
                    hp_ref, wg_hbm, wu_hbm, wd_hbm, y2_ref,
                    wgf, wuf, wdf, wgb, wub, wdb, xbuf, ybuf, wsem, gsem, ssem, *, layer):
    tm = MOE_TM
    j = pl.program_id(0)
    cur = j % 2

    def weight_copies(e, s):
        return [_row_copy(wg_hbm.at[layer, e], wgf.at[s], wsem.at[s]),
                _row_copy(wu_hbm.at[layer, e], wuf.at[s], wsem.at[s]),
                _row_copy(wd_hbm.at[layer, e], wdf.at[s], wsem.at[s])]

    def for_rows(fn, unrolled):
        if unrolled:
            for r in range(tm):
                fn(r)
        else:
            lax.fori_loop(0, tm, lambda r, c: (fn(r), c)[1], 0)

    def start_gather(tile, slot, unrolled):
        def one(r):
            tok = map_ref[tile * tm + r] & (TOKENS - 1)
            src = hp_ref.at[pl.ds(pl.multiple_of(tok * SUBLANES, SUBLANES), SUBLANES), :]
            _row_copy(src, xbuf.at[slot, pl.ds(pl.multiple_of(r * SUBLANES, SUBLANES), SUBLANES), :],
                      gsem.at[slot]).start()
        for_rows(one, unrolled)

    def start_scatter(tile, slot, unrolled):
        def one(r):
            dst = map_ref[tile * tm + r] >> TOKEN_BITS
            _row_copy(ybuf.at[slot, pl.ds(pl.multiple_of(r * SUBLANES, SUBLANES), SUBLANES), :],
                      y2_ref.at[pl.ds(pl.multiple_of(dst * SUBLANES, SUBLANES), SUBLANES), :],
                      ssem.at[slot]).start()
        for_rows(one, unrolled)

    def wait_gather(slot):
        _row_copy(hp_ref.at[pl.ds(0, tm * SUBLANES), :], xbuf.at[slot], gsem.at[slot]).wait()

    def wait_scatter(slot):
        _row_copy(ybuf.at[slot], y2_ref.at[pl.ds(0, tm * SUBLANES), :], ssem.at[slot]).wait()

    @pl.when(j == 0)
    def _():
        ybuf[...] = jnp.zeros_like(ybuf)
        for cp in weight_copies(te_ref[0], 0):
            cp.start(priority=WEIGHT_DMA_PRIORITY)
        start_gather(0, 0, False)
        start_scatter(MOE_TILES + 1, 0, False)

    @pl.when(pv_ref[j] == 1)
    def _():
        wait_gather(cur)

    @pl.when((tv_ref[j] == 1) & (tf_ref[j] == 1))
    def _():
        s = ws_ref[j]

        @pl.when(nx_ref[j] >= 0)
        def _():
            for cp in weight_copies(nx_ref[j], 1 - s):
                cp.start(priority=WEIGHT_DMA_PRIORITY)

        for cp in weight_copies(te_ref[j], s):
            cp.wait()
        wgb[...] = wgf[s].astype(BF16)
        wub[...] = wuf[s].astype(BF16)
        wdb[...] = wdf[s].astype(BF16)

    @pl.when(tv_ref[j] == 1)
    def _():
        start_gather(jnp.minimum(j + 1, MOE_TILES - 1), 1 - cur, True)
        start_scatter(jnp.where(j == 0, MOE_TILES, j - 1), 1 - cur, True)
        x = _from_row_tiles(xbuf.at[cur], tm).astype(BF16)
        gate = jnp.dot(x, wgb[...], preferred_element_type=F32)
        up = jnp.dot(x, wub[...], preferred_element_type=F32)
        a = ((gate * jax.nn.sigmoid(gate)) * up).astype(BF16)
        y = jnp.dot(a, wdb[...], preferred_element_type=F32)
        wait_scatter(cur)
        _to_row_tiles(ybuf.at[cur], y, tm)

    @pl.when((tv_ref[j] == 0) & (pv_ref[j] == 1))
    def _():
        start_scatter(j - 1, 1 - cur, False)
        wait_scatter(cur)
        wait_scatter(1 - cur)


def _experts(plan, row_map, h_rows, w_gate, w_up, w_down, layer):
    tm = MOE_TM
    any_spec = pl.BlockSpec(memory_space=pl.ANY)
    grid_spec = pltpu.PrefetchScalarGridSpec(
        num_scalar_prefetch=7,
        grid=(MOE_TILES,),
        in_specs=[any_spec, any_spec, any_spec, any_spec],
        out_specs=any_spec,
        scratch_shapes=[pltpu.VMEM((2, D_MODEL, EXPERT_HIDDEN), F32),
                        pltpu.VMEM((2, D_MODEL, EXPERT_HIDDEN), F32),
                        pltpu.VMEM((2, EXPERT_HIDDEN, D_MODEL), F32),
                        pltpu.VMEM((D_MODEL, EXPERT_HIDDEN), BF16),
                        pltpu.VMEM((D_MODEL, EXPERT_HIDDEN), BF16),
                        pltpu.VMEM((EXPERT_HIDDEN, D_MODEL), BF16),
                        pltpu.VMEM((2, tm * SUBLANES, LANES), U32),
                        pltpu.VMEM((2, tm * SUBLANES, LANES), U32),
                        pltpu.SemaphoreType.DMA((2,)),
                        pltpu.SemaphoreType.DMA((2,)),
                        pltpu.SemaphoreType.DMA((2,))])
    return pl.pallas_call(
        functools.partial(_experts_kernel, layer=layer),
        grid_spec=grid_spec,
        out_shape=jax.ShapeDtypeStruct((OUT_ROWS * SUBLANES, LANES), U32),
        compiler_params=_cparams(1, 56),
        name="moe_experts",
    )(*plan, row_map, h_rows, w_gate, w_up, w_down)


COMBINE_TM = 256


def _combine_kernel(x_ref, gate_ref, wcol_ref, y0_ref, y1_ref, *refs, final):
    if final:
        fg_ref, o_ref = refs
    else:
        ng_ref, nsc_ref, nsh_ref, o_ref, h_ref = refs
    tm = COMBINE_TM
    y0 = _from_row_tiles(y0_ref, tm)
    y1 = _from_row_tiles(y1_ref, tm)
    out = x_ref[...] + gate_ref[0] * (wcol_ref[:, 0:1] * y0 + wcol_ref[:, 1:2] * y1)
    if final:
        o_ref[...] = (out * lax.rsqrt(jnp.mean(out * out, axis=-1, keepdims=True) + EPS)) * fg_ref[...]
    else:
        o_ref[...] = out
        h_ref[...] = _norm_mod(out, ng_ref[...], nsc_ref[0], nsh_ref[0]).astype(h_ref.dtype)


def _combine(x2d, gate, wcol, y2, final_g=None, next_norm=None):
    tm = COMBINE_TM
    per_batch = SEQ // tm
    final = final_g is not None
    row = pl.BlockSpec((tm, D_MODEL), lambda i: (i, 0))
    vec = pl.BlockSpec((1, D_MODEL), lambda i: (0, 0))
    mod = pl.BlockSpec((1, 1, D_MODEL), lambda i: (i // per_batch, 0, 0))
    slot0 = pl.BlockSpec((tm * SUBLANES, LANES), lambda i: (i, 0))
    slot1 = pl.BlockSpec((tm * SUBLANES, LANES), lambda i: (TOKENS // tm + i, 0))
    extra_specs, extra = ([vec], [final_g]) if final else ([vec, mod, mod], list(next_norm))
    out_f32 = jax.ShapeDtypeStruct((TOKENS, D_MODEL), F32)
    return pl.pallas_call(
        functools.partial(_combine_kernel, final=final),
        grid=(TOKENS // tm,),
        in_specs=[row, mod, pl.BlockSpec((tm, LANES), lambda i: (i, 0)), slot0, slot1] + extra_specs,
        out_specs=row if final else [row, row],
        out_shape=out_f32 if final else [out_f32, jax.ShapeDtypeStruct((TOKENS, D_MODEL), BF16)],
        compiler_params=_cparams(1, 48),
        name="moe_combine",
    )(x2d, gate, wcol, y2, y2, *extra)


def _moe_plan(counts):
    counts = counts.astype(I32)
    tiles = (counts + (MOE_TM - 1)) // MOE_TM
    cum = jnp.cumsum(tiles)
    first_tile = cum - tiles
    n_used = cum[-1]
    j = jnp.arange(MOE_TILES, dtype=I32)
    te = jnp.minimum(jnp.sum((j[:, None] >= cum[None, :]).astype(I32), axis=1), N_EXPERTS - 1)
    valid = j < n_used
    te = jnp.where(valid, te, jnp.take(te, n_used - 1))
    tf = (valid & (j == jnp.take(first_tile, te))).astype(I32)
    tv = valid.astype(I32)
    pv = jnp.concatenate([jnp.ones((1,), I32), tv[:-1]])
    has_rows = tiles > 0
    ordinal = jnp.cumsum(has_rows.astype(I32)) - 1
    e_ids = jnp.arange(N_EXPERTS, dtype=I32)
    later = (e_ids[None, :] > e_ids[:, None]) & has_rows[None, :]
    succ = jnp.min(jnp.where(later, e_ids[None, :], N_EXPERTS), axis=1)
    succ = jnp.where(succ == N_EXPERTS, -1, succ)
    nx = jnp.take(succ, te)
    ws = jnp.take(ordinal, te) % 2
    return (te, tf, tv, pv, nx, ws), first_tile * MOE_TM


def _row_map(pos0, pos1):
    p = jnp.arange(MAP_TILES * MOE_TM, dtype=I32)
    tile = p // MOE_TM
    odd = jnp.where(tile < MOE_TILES, tile % 2, jnp.where(tile == MOE_TILES, 1, 0))
    trash = 2 * TOKENS + odd * MOE_TM + p % MOE_TM
    tok = jnp.arange(TOKENS, dtype=I32)
    vals = jnp.concatenate([tok | (tok << TOKEN_BITS), tok | ((TOKENS + tok) << TOKEN_BITS)])
    return (trash << TOKEN_BITS).at[jnp.concatenate([pos0, pos1])].set(vals, unique_indices=True)


def _positions_kernel(ei_ref, off_ref, pos_ref):
    tm = ei_ref.shape[1]
    rows32 = lax.broadcasted_iota(I32, (N_EXPERTS, tm), 0)
    rows8 = lax.broadcasted_iota(I32, (SUBLANES, tm), 0)
    off = off_ref[:, 0:1]

    def pos(k):
        first = jnp.sum(jnp.where(rows32 == ei_ref[k:k + 1, :], off, 0), axis=0, keepdims=True)
        return first + ei_ref[k + 2:k + 3, :]

    pos_ref[...] = jnp.where(rows8 == 0, pos(0), jnp.where(rows8 == 1, pos(1), 0))


def _positions(ei, row_off):
    tm = 2048
    blk = pl.BlockSpec((SUBLANES, tm), lambda i: (0, i))
    return pl.pallas_call(
        _positions_kernel,
        grid=(TOKENS // tm,),
        in_specs=[blk, pl.BlockSpec((N_EXPERTS, LANES), lambda i: (0, 0))],
        out_specs=blk,
        out_shape=jax.ShapeDtypeStruct((SUBLANES, TOKENS), I32),
        compiler_params=_cparams(1, 32),
        name="moe_positions",
    )(ei, jnp.broadcast_to(row_off[:, None], (N_EXPERTS, LANES)))


def _moe_layer(x2d, layer, mods, norm_g, w_group, b_group, w_expert, b_expert, w_gate, w_up, w_down,
               final_g=None, next_norm=None):
    shift, scale, gate = mods
    w_t = jnp.concatenate([w_expert.T, w_group.T,
                           jnp.zeros((ROUTER_ROWS - N_EXPERTS - N_GROUPS, D_MODEL), F32)], axis=0)
    b_rows = jnp.concatenate([b_expert, b_group, jnp.zeros((ROUTER_ROWS - N_EXPERTS - N_GROUPS,), F32)])
    b_rows = jnp.broadcast_to(b_rows[:, None], (ROUTER_ROWS, ROUTER_TM))
    ei, wcol, cnt, h_rows = _router(x2d, norm_g, scale, shift, w_t, b_rows)
    plan, row_off = _moe_plan(cnt[:, 0])
    pos = _positions(ei, row_off)
    y2 = _experts(plan, _row_map(pos[0], pos[1]), h_rows, w_gate, w_up, w_down, layer)
    return _combine(x2d, gate, wcol, y2, final_g=final_g, next_norm=next_norm)


def _split_mods(mod, layer):
    m = mod[layer]
    return tuple(m[:, k * D_MODEL:(k + 1) * D_MODEL].reshape(BATCH, 1, D_MODEL) for k in range(3))


def kernel(x, c, positions, w_mod_mix, b_mod_mix, norm_mix, w_mod_ffn, b_mod_ffn, norm_ffn, ev_w_in, ev_q_norm, ev_w_q_up, ev_kv_norm, ev_w_kv_up, ev_w_out, od_w_in, od_w_out, moe_w_group, moe_b_group, moe_w_expert, moe_b_expert, moe_w_gate, moe_w_up, moe_w_down, final_norm):
    x2d = x.reshape(TOKENS, D_MODEL)
    c_lanes = jnp.broadcast_to(c[:, :, None], (BATCH, D_MODEL, LANES))
    mod_mix = _mods(c_lanes, w_mod_mix, b_mod_mix)
    mod_ffn = _mods(c_lanes, w_mod_ffn, b_mod_ffn)
    pos_lanes = jnp.broadcast_to(positions.reshape(TOKENS, 1), (TOKENS, LANES))
    cos_m, sin_m, cos_r, sin_r = _rope_tables(pos_lanes)
    final_g = final_norm.reshape(1, D_MODEL)

    shift, scale, gate = _split_mods(mod_mix, 0)
    half = MLA_ROPE // 2
    w_in = ev_w_in[0]
    c_kr0 = 3 * SB_WIDTH + MLA_Q_LORA + MLA_KV_LORA
    zc = lambda n: jnp.zeros((D_MODEL, n), F32)
    w_in_pad = jnp.concatenate([w_in[:, :c_kr0], w_in[:, c_kr0:c_kr0 + half], zc(half),
                                w_in[:, c_kr0 + half:], zc(half), zc(LANES)], axis=1).astype(BF16)
    h = _norm_mod_rows(x2d, norm_mix[0].reshape(1, D_MODEL), scale, shift)
    proj = _proj([h], w_in_pad, 1024, 48)
    wq = ev_w_q_up[0].reshape(MLA_Q_LORA, MLA_HEADS, MLA_NOPE + MLA_ROPE)
    zq = jnp.zeros((MLA_Q_LORA, MLA_HEADS, half), F32)
    wq = jnp.concatenate([wq[:, :, :MLA_NOPE], wq[:, :, MLA_NOPE:MLA_NOPE + half], zq,
                          wq[:, :, MLA_NOPE + half:], zq], axis=2).reshape(MLA_Q_LORA, MLA_HEADS * MLA_QK_PAD)
    wkv = ev_w_kv_up[0].reshape(MLA_KV_LORA, MLA_HEADS, MLA_NOPE + MLA_V)
    wkv = jnp.concatenate([wkv[:, :, :MLA_NOPE].reshape(MLA_KV_LORA, MLA_HEADS * MLA_NOPE),
                           wkv[:, :, MLA_NOPE:].reshape(MLA_KV_LORA, MLA_HEADS * MLA_V)], axis=1)
    q_cat, k_cat, v_mla = _mla_prep(proj, ev_q_norm[0].reshape(1, MLA_Q_LORA),
                                    ev_kv_norm[0].reshape(1, MLA_KV_LORA), wq, wkv, cos_m, sin_m)
    o_sb = _sb_attention(proj)
    o_mla = _mla_attention(q_cat, k_cat, v_mla)
    x2d = _proj([o_sb, o_mla], ev_w_out[0], 512, 40, x2d, gate)
    shift, scale, gate = _split_mods(mod_mix, 1)
    x2d, h = _moe_layer(x2d, 0, _split_mods(mod_ffn, 0), norm_ffn[0].reshape(1, D_MODEL),
                        moe_w_group[0], moe_b_group[0], moe_w_expert[0], moe_b_expert[0],
                        moe_w_gate, moe_w_up, moe_w_down,
                        next_norm=(norm_mix[1].reshape(1, D_MODEL), scale, shift))

    proj = _proj([h], od_w_in[0], 1024, 48)
    o_ret = _retention(proj, cos_r, sin_r)
    x2d = _proj([o_ret], od_w_out[0], 512, 56, x2d, gate)
    out = _moe_layer(x2d, 1, _split_mods(mod_ffn, 1), norm_ffn[1].reshape(1, D_MODEL),
                     moe_w_group[1], moe_b_group[1], moe_w_expert[1], moe_b_expert[1],
                     moe_w_gate, moe_w_up, moe_w_down, final_g=final_g)
    return out.reshape(BATCH, SEQ, D_MODEL)
```

```python
import functools
import math

import jax
import jax.numpy as jnp
from jax import lax
from jax.experimental import pallas as pl
from jax.experimental.pallas import tpu as pltpu

F32 = jnp.float32
BF16 = jnp.bfloat16
I32 = jnp.int32

D_MODEL = 2048
BATCH = 4
SEQ = 2048
DEPTH = 2
TOKENS = BATCH * SEQ

HEAD_DIM = 128
SB_HEADS = 8
SB_WIDTH = SB_HEADS * HEAD_DIM
MLA_HEADS = 8
MLA_Q_LORA = 512
MLA_KV_LORA = 256
MLA_NOPE = 128
MLA_ROPE = 64
MLA_V = 128
MLA_QK_PAD = 256
EVEN_IN_PAD = 4096
RET_HEADS = 8
RET_QK = 256
RET_V = 512
RET_QK_WIDTH = RET_HEADS * RET_QK
RET_V_WIDTH = RET_HEADS * RET_V
ODD_IN_WIDTH = 2 * RET_QK_WIDTH + 2 * RET_V_WIDTH
N_GROUPS = 4
EXPERTS_PER_GROUP = 8
N_EXPERTS = N_GROUPS * EXPERTS_PER_GROUP
EXPERT_HIDDEN = 512
CHUNK = 128
ROPE_BASE = 10000.0
EPS = 1e-6

LANES = 128
SUBLANES = 8
HALF_D = D_MODEL // 2
assert HALF_D == SUBLANES * LANES
U32 = jnp.uint32

ROUTER_ROWS = 48
MOE_TM = 256
MOE_TILES = (2 * TOKENS) // MOE_TM + N_EXPERTS
MOE_ROWS = MOE_TILES * MOE_TM
ZERO_ROWS = MOE_TM // 2

NT_DIMS = (((1,), (1,)), ((), ()))


def _cparams(n_grid, vmem_mb):
    return pltpu.CompilerParams(dimension_semantics=("arbitrary",) * n_grid,
                                vmem_limit_bytes=vmem_mb * 1024 * 1024)


def _norm_mod(x, g, scale, shift):
    y = x * lax.rsqrt(jnp.mean(x * x, axis=-1, keepdims=True) + EPS)
    return (y * g) * (1.0 + scale) + shift


def _to_row_tiles(ref_at, val, rows):
    for b in range(SUBLANES):
        lo = val[:, b * LANES:(b + 1) * LANES]
        hi = val[:, HALF_D + b * LANES:HALF_D + (b + 1) * LANES]
        ref_at[pl.ds(b, rows, stride=SUBLANES), :] = pltpu.pack_elementwise([lo, hi], packed_dtype=BF16)


def _from_row_tiles(ref_at, rows):
    lo, hi = [], []
    for b in range(SUBLANES):
        w = ref_at[pl.ds(b, rows, stride=SUBLANES), :]
        lo.append(pltpu.unpack_elementwise(w, index=0, packed_dtype=BF16, unpacked_dtype=F32))
        hi.append(pltpu.unpack_elementwise(w, index=1, packed_dtype=BF16, unpacked_dtype=F32))
    return jnp.concatenate(lo + hi, axis=1)


def _mods_kernel(c_ref, w_ref, b_ref, o_ref, cond_ref):
    @pl.when((pl.program_id(0) == 0) & (pl.program_id(1) == 0))
    def _():
        c = c_ref[...]
        cond_ref[...] = c * jax.nn.sigmoid(c)

    tn = w_ref.shape[2]
    rows = []
    for b in range(BATCH):
        cb = cond_ref[b]
        cols = [jnp.sum(w_ref[0, :, j * LANES:(j + 1) * LANES] * cb, axis=0, keepdims=True)
                for j in range(tn // LANES)]
        rows.append(jnp.concatenate(cols, axis=1))
    o_ref[0] = jnp.concatenate(rows, axis=0) + b_ref[0]


def _mods(c_lanes, w_mod, b_mod):
    tn = 512
    n3 = 3 * D_MODEL
    return pl.pallas_call(
        _mods_kernel,
        grid=(DEPTH, n3 // tn),
        in_specs=[pl.BlockSpec((BATCH, D_MODEL, LANES), lambda l, j: (0, 0, 0)),
                  pl.BlockSpec((1, D_MODEL, tn), lambda l, j: (l, 0, j)),
                  pl.BlockSpec((1, 1, tn), lambda l, j: (l, 0, j))],
        out_specs=pl.BlockSpec((1, BATCH, tn), lambda l, j: (l, 0, j)),
        out_shape=jax.ShapeDtypeStruct((DEPTH, BATCH, n3), F32),
        scratch_shapes=[pltpu.VMEM((BATCH, D_MODEL, LANES), F32)],
        compiler_params=_cparams(2, 32),
        name="adaln_mods",
    )(c_lanes, w_mod, b_mod.reshape(DEPTH, 1, n3))


def _rope_tables_kernel(pos_ref, fm_ref, sg_ref, fr_ref, cm_ref, sm_ref, cr_ref, sr_ref):
    p = pos_ref[...].astype(F32)
    am = p * fm_ref[...]
    cm_ref[...] = jnp.cos(am)
    sm_ref[...] = jnp.sin(am) * sg_ref[...]
    ar = p * fr_ref[...]
    cr_ref[...] = jnp.cos(ar)
    sr_ref[...] = jnp.sin(ar)


def _rope_tables(pos_lanes):
    half = MLA_ROPE // 2
    f_mla = jnp.exp(-math.log(ROPE_BASE) * jnp.arange(half, dtype=F32) / half)
    z = jnp.zeros((half,), F32)
    fm = jnp.concatenate([f_mla, z, f_mla, z]).reshape(1, LANES)
    sg = jnp.concatenate([-jnp.ones((half,), F32), z, jnp.ones((half,), F32), z]).reshape(1, LANES)
    hr = RET_QK // 2
    fr = jnp.exp(-math.log(ROPE_BASE) * jnp.arange(hr, dtype=F32) / hr).reshape(1, LANES)
    tm = 1024
    row = pl.BlockSpec((tm, LANES), lambda i: (i, 0))
    vec = pl.BlockSpec((1, LANES), lambda i: (0, 0))
    tab = jax.ShapeDtypeStruct((TOKENS, LANES), F32)
    return pl.pallas_call(
        _rope_tables_kernel,
        grid=(TOKENS // tm,),
        in_specs=[row, vec, vec, vec],
        out_specs=[row, row, row, row],
        out_shape=[tab, tab, tab, tab],
        compiler_params=_cparams(1, 32),
        name="rope_tables",
    )(pos_lanes, fm, sg, fr)


def _norm_mod_kernel(x_ref, g_ref, sc_ref, sh_ref, o_ref):
    o_ref[...] = _norm_mod(x_ref[...], g_ref[...], sc_ref[0], sh_ref[0]).astype(o_ref.dtype)


def _norm_mod_rows(x2d, g, scale, shift):
    tm = 512
    per_batch = SEQ // tm
    mod = pl.BlockSpec((1, 1, D_MODEL), lambda i: (i // per_batch, 0, 0))
    row = pl.BlockSpec((tm, D_MODEL), lambda i: (i, 0))
    return pl.pallas_call(
        _norm_mod_kernel,
        grid=(TOKENS // tm,),
        in_specs=[row, pl.BlockSpec((1, D_MODEL), lambda i: (0, 0)), mod, mod],
        out_specs=row,
        out_shape=jax.ShapeDtypeStruct((TOKENS, D_MODEL), BF16),
        compiler_params=_cparams(1, 32),
        name="norm_mod",
    )(x2d, g, scale, shift)


PROJ_TM = 1024


def _proj_kernel(*refs, n_in, residual):
    lhs_refs = refs[:n_in]
    if residual:
        w_ref, x_ref, gate_ref, o_ref, wb_ref = refs[n_in:]
    else:
        w_ref, o_ref, wb_ref = refs[n_in:]

    @pl.when(pl.program_id(1) == 0)
    def _():
        wb_ref[...] = w_ref[...].astype(BF16)

    acc = None
    off = 0
    for y_ref in lhs_refs:
        kk = y_ref.shape[1]
        d = jnp.dot(y_ref[...], wb_ref[off:off + kk, :], preferred_element_type=F32)
        acc = d if acc is None else acc + d
        off += kk
    if residual:
        o_ref[...] = x_ref[...] + gate_ref[0] * acc
    else:
        o_ref[...] = acc.astype(o_ref.dtype)


def _proj(ys, w, tn, vmem_mb, x2d=None, gate=None):
    tm = PROJ_TM
    k, n = w.shape
    per_batch = SEQ // tm
    residual = x2d is not None
    in_specs = [pl.BlockSpec((tm, y.shape[1]), lambda j, i: (i, 0)) for y in ys]
    in_specs.append(pl.BlockSpec((k, tn), lambda j, i: (0, j)))
    args = list(ys) + [w]
    if residual:
        in_specs += [pl.BlockSpec((tm, tn), lambda j, i: (i, j)),
                     pl.BlockSpec((1, 1, tn), lambda j, i: (i // per_batch, 0, j))]
        args += [x2d, gate]
    return pl.pallas_call(
        functools.partial(_proj_kernel, n_in=len(ys), residual=residual),
        grid=(n // tn, TOKENS // tm),
        in_specs=in_specs,
        out_specs=pl.BlockSpec((tm, tn), lambda j, i: (i, j)),
        out_shape=jax.ShapeDtypeStruct((TOKENS, n), F32 if residual else BF16),
        scratch_shapes=[pltpu.VMEM((k, tn), BF16)],
        compiler_params=_cparams(2, vmem_mb),
        name="out_proj" if residual else "in_proj",
    )(*args)


def _mla_prep_kernel(cq_ref, ckv_ref, kr_ref, qn_ref, kvn_ref, wq_ref, wkv_ref, cos_ref, sin_ref,
                     q_ref, k_ref, v_ref):
    cos = cos_ref[...]
    sin = sin_ref[...]

    def rope(x):
        return x * cos + pltpu.roll(x, LANES // 2, 1) * sin

    def rms(x, g):
        xf = x.astype(F32)
        return (xf * lax.rsqrt(jnp.mean(xf * xf, axis=-1, keepdims=True) + EPS)) * g

    cq = rms(cq_ref[...], qn_ref[...]).astype(BF16)
    q = jnp.dot(cq, wq_ref[...].astype(BF16), preferred_element_type=F32)
    ckv = rms(ckv_ref[...], kvn_ref[...]).astype(BF16)
    kv = jnp.dot(ckv, wkv_ref[...].astype(BF16), preferred_element_type=F32)
    kr = rope(kr_ref[...].astype(F32)).astype(BF16)
    scale = (MLA_NOPE + MLA_ROPE) ** -0.5
    for h in range(MLA_HEADS):
        c0 = h * MLA_QK_PAD
        q_ref[:, c0:c0 + MLA_NOPE] = (q[:, c0:c0 + MLA_NOPE] * scale).astype(BF16)
        q_ref[:, c0 + MLA_NOPE:c0 + MLA_QK_PAD] = (
            rope(q[:, c0 + MLA_NOPE:c0 + MLA_QK_PAD]) * scale).astype(BF16)
        k_ref[:, c0:c0 + MLA_NOPE] = kv[:, h * MLA_NOPE:(h + 1) * MLA_NOPE].astype(BF16)
        k_ref[:, c0 + MLA_NOPE:c0 + MLA_QK_PAD] = kr
    v_ref[...] = kv[:, MLA_HEADS * MLA_NOPE:].astype(BF16)


def _mla_prep(proj, q_norm, kv_norm, wq, wkv, cos_m, sin_m):
    tm = 512
    c_q0 = 3 * SB_WIDTH
    c_kv0 = c_q0 + MLA_Q_LORA
    c_kr0 = c_kv0 + MLA_KV_LORA
    full = lambda shape: pl.BlockSpec(shape, lambda i: (0, 0))
    row = lambda w: pl.BlockSpec((tm, w), lambda i: (i, 0))
    qk_w = MLA_HEADS * MLA_QK_PAD
    return pl.pallas_call(
        _mla_prep_kernel,
        grid=(TOKENS // tm,),
        in_specs=[pl.BlockSpec((tm, MLA_Q_LORA), lambda i: (i, c_q0 // MLA_Q_LORA)),
                  pl.BlockSpec((tm, MLA_KV_LORA), lambda i: (i, c_kv0 // MLA_KV_LORA)),
                  pl.BlockSpec((tm, LANES), lambda i: (i, c_kr0 // LANES)),
                  full((1, MLA_Q_LORA)), full((1, MLA_KV_LORA)),
                  full((MLA_Q_LORA, qk_w)), full((MLA_KV_LORA, 2 * MLA_HEADS * MLA_NOPE)),
                  row(LANES), row(LANES)],
        out_specs=[row(qk_w), row(qk_w), row(MLA_HEADS * MLA_V)],
        out_shape=[jax.ShapeDtypeStruct((TOKENS, qk_w), BF16),
                   jax.ShapeDtypeStruct((TOKENS, qk_w), BF16),
                   jax.ShapeDtypeStruct((TOKENS, MLA_HEADS * MLA_V), BF16)],
        compiler_params=_cparams(1, 48),
        name="mla_prep",
    )(proj, proj, proj, q_norm, kv_norm, wq, wkv, cos_m, sin_m)


ATT_T = 256
ATT_HEADS = 2
LOG2E = math.log2(math.e)
SB_SKIP_LOG2 = -160.0


def _sb_attn_kernel(q_ref, k_ref, v_ref, o_ref, acc_ref, run_ref):
    t = ATT_T
    ii = lax.broadcasted_iota(I32, (t, t), 0)
    jj = lax.broadcasted_iota(I32, (t, t), 1)
    strict = jj < ii
    upper2 = (lax.broadcasted_iota(I32, (2 * t, t), 0) % t > lax.broadcasted_iota(I32, (2 * t, t), 1)
              ).astype(BF16)
    z_scale = HEAD_DIM ** -0.5 * LOG2E

    def block(q, kb, cols, run, diag):
        rows = pl.ds(pl.multiple_of(kb * t, t), t)
        z = lax.dot_general(q, k_ref[rows, cols], NT_DIMS, preferred_element_type=F32) * z_scale
        log_beta = jnp.minimum(z, 0.0) - jnp.log2(1.0 + jnp.exp2(jnp.minimum(z, -z)))
        log_keep = log_beta - z
        if diag:
            log_keep = jnp.where(strict, log_keep, 0.0)
        hi = log_keep.astype(BF16)
        lo = (log_keep - hi.astype(F32)).astype(BF16)
        within = jnp.dot(jnp.concatenate([hi, lo], axis=1), upper2, preferred_element_type=F32)
        a = jnp.exp2(log_beta + (within + run))
        if diag:
            a = jnp.where(strict, a, 0.0)
        pv = jnp.dot(a.astype(BF16), v_ref[rows, cols], preferred_element_type=F32)
        return pv, run + (within[:, 0:1] + log_keep[:, 0:1])

    nq = SEQ // t
    for qi in range(nq):
        qrows = pl.ds(qi * t, t)
        for hh in range(ATT_HEADS):
            cols = slice(hh * HEAD_DIM, (hh + 1) * HEAD_DIM)
            q = q_ref[qrows, cols]
            acc, run = block(q, qi, cols, jnp.zeros((t, 1), F32), True)
            if qi > 0:
                pv, run = block(q, qi - 1, cols, run, False)
                acc = acc + pv
            acc_ref[qrows, cols] = acc
            run_ref[qrows, cols] = jnp.broadcast_to(run, (t, HEAD_DIM))

    def more(qrows):
        return (jnp.max(run_ref[qrows, :]) > SB_SKIP_LOG2).astype(I32)

    for qi in range(2, nq):
        qrows = pl.ds(qi * t, t)

        def body(c, qrows=qrows):
            kb = c[0]
            for hh in range(ATT_HEADS):
                cols = slice(hh * HEAD_DIM, (hh + 1) * HEAD_DIM)
                pv, run = block(q_ref[qrows, cols], kb, cols, run_ref[qrows, hh * HEAD_DIM:hh * HEAD_DIM + 1],
                                False)
                acc_ref[qrows, cols] = acc_ref[qrows, cols] + pv
                run_ref[qrows, cols] = jnp.broadcast_to(run, (t, HEAD_DIM))
            return kb - 1, more(qrows)

        lax.while_loop(lambda c: (c[0] >= 0) & (c[1] == 1), body, (jnp.int32(qi - 2), more(qrows)))

    o_ref[...] = acc_ref[...].astype(o_ref.dtype)


def _sb_attention(proj):
    w = ATT_HEADS * HEAD_DIM
    blk = lambda off: pl.BlockSpec((SEQ, w), lambda b, h: (b, off + h))
    n = SB_HEADS // ATT_HEADS
    return pl.pallas_call(
        _sb_attn_kernel,
        grid=(BATCH, n),
        in_specs=[blk(0), blk(n), blk(2 * n)],
        out_specs=pl.BlockSpec((SEQ, w), lambda b, h: (b, h)),
        out_shape=jax.ShapeDtypeStruct((TOKENS, SB_WIDTH), BF16),
        scratch_shapes=[pltpu.VMEM((SEQ, w), F32), pltpu.VMEM((SEQ, w), F32)],
        compiler_params=_cparams(2, 48),
        name="sb_attention",
    )(proj, proj, proj)


def _mla_attn_kernel(q_ref, k_ref, v_ref, o_ref):
    t = ATT_T
    ii = lax.broadcasted_iota(I32, (t, t), 0)
    jj = lax.broadcasted_iota(I32, (t, t), 1)
    causal = jj <= ii
    neg = -1e30

    def step(q, kb, hh, carry, diag):
        m, l, acc = carry
        rows = pl.ds(pl.multiple_of(kb * t, t), t)
        s = lax.dot_general(q, k_ref[rows, hh * MLA_QK_PAD:(hh + 1) * MLA_QK_PAD], NT_DIMS,
                            preferred_element_type=F32)
        if diag:
            s = jnp.where(causal, s, neg)
        m_new = jnp.maximum(m, jnp.max(s, axis=1, keepdims=True))
        alpha = jnp.exp(m - m_new)
        p = jnp.exp(s - m_new)
        l = alpha * l + jnp.sum(p, axis=1, keepdims=True)
        acc = alpha * acc + jnp.dot(p.astype(BF16), v_ref[rows, hh * MLA_V:(hh + 1) * MLA_V],
                                    preferred_element_type=F32)
        return m_new, l, acc

    for qi in range(SEQ // t):
        qrows = pl.ds(qi * t, t)
        for hh in range(ATT_HEADS):
            q = q_ref[qrows, hh * MLA_QK_PAD:(hh + 1) * MLA_QK_PAD]
            c = (jnp.full((t, 1), neg, F32), jnp.zeros((t, 1), F32), jnp.zeros((t, MLA_V), F32))
            for kb in range(qi):
                c = step(q, kb, hh, c, False)
            _, l, acc = step(q, qi, hh, c, True)
            o_ref[qrows, hh * MLA_V:(hh + 1) * MLA_V] = (acc / l).astype(o_ref.dtype)


def _mla_attention(q_cat, k_cat, v):
    qk = pl.BlockSpec((SEQ, ATT_HEADS * MLA_QK_PAD), lambda b, h: (b, h))
    vo = pl.BlockSpec((SEQ, ATT_HEADS * MLA_V), lambda b, h: (b, h))
    return pl.pallas_call(
        _mla_attn_kernel,
        grid=(BATCH, MLA_HEADS // ATT_HEADS),
        in_specs=[qk, qk, vo],
        out_specs=vo,
        out_shape=jax.ShapeDtypeStruct((TOKENS, MLA_HEADS * MLA_V), BF16),
        compiler_params=_cparams(2, 48),
        name="mla_attention",
    )(q_cat, k_cat, v)


def _retention_kernel(lg_ref, q_ref, k_ref, v_ref, g_ref, cos_ref, sin_ref, o_ref, state_ref):
    lg = lg_ref[pl.program_id(1)]
    c = CHUNK
    half = RET_QK // 2
    ii = lax.broadcasted_iota(I32, (c, c), 0)
    jj = lax.broadcasted_iota(I32, (c, c), 1)
    rel = (ii - jj).astype(F32)
    intra = jnp.where(rel >= 0, jnp.exp(jnp.maximum(rel, 0.0) * lg), 0.0)
    idx = lax.broadcasted_iota(I32, (c, 1), 0).astype(F32)
    q_decay = jnp.exp((idx + 1.0) * lg)
    k_decay = jnp.exp((c - 1.0 - idx) * lg)
    chunk_decay = jnp.exp(jnp.full((1, 1), float(c), F32) * lg)
    state_ref[...] = jnp.zeros_like(state_ref)

    def rope(x, cos, sin):
        x1, x2 = x[:, :half], x[:, half:]
        return jnp.concatenate([x1 * cos - x2 * sin, x1 * sin + x2 * cos], axis=1)

    for ci in range(SEQ // c):
        rows = pl.ds(ci * c, c)
        cos = cos_ref[rows, :]
        sin = sin_ref[rows, :]
        qr = rope(q_ref[rows, :].astype(F32), cos, sin)
        kr = rope(k_ref[rows, :].astype(F32), cos, sin) * (RET_QK ** -0.5)
        v = v_ref[rows, :]
        qb = qr.astype(BF16)
        scores = lax.dot_general(qb, kr.astype(BF16), NT_DIMS, preferred_element_type=F32) * intra
        inner = jnp.dot(scores.astype(BF16), v, preferred_element_type=F32)
        st = state_ref[...]
        cross = jnp.dot(qb, st.astype(BF16), preferred_element_type=F32) * q_decay
        kd_t = jnp.transpose(kr * k_decay).astype(BF16)
        state_ref[...] = st * chunk_decay + jnp.dot(kd_t, v, preferred_element_type=F32)
        o = inner + cross
        o = o * lax.rsqrt(jnp.mean(o * o, axis=-1, keepdims=True) + EPS)
        gg = g_ref[rows, :].astype(F32)
        o_ref[rows, :] = ((gg * jax.nn.sigmoid(gg)) * o).astype(o_ref.dtype)


def _retention(proj, cos_r, sin_r):
    log_gamma = jnp.log1p(-jnp.exp2(-5.0 - jnp.arange(RET_HEADS, dtype=F32)))
    k0 = RET_QK_WIDTH // RET_QK
    v0 = 2 * RET_QK_WIDTH // RET_V
    g0 = (2 * RET_QK_WIDTH + RET_V_WIDTH) // RET_V
    tab = pl.BlockSpec((SEQ, LANES), lambda b, h: (b, 0))
    return pl.pallas_call(
        _retention_kernel,
        grid=(BATCH, RET_HEADS),
        in_specs=[pl.BlockSpec(memory_space=pltpu.SMEM),
                  pl.BlockSpec((SEQ, RET_QK), lambda b, h: (b, h)),
                  pl.BlockSpec((SEQ, RET_QK), lambda b, h: (b, k0 + h)),
                  pl.BlockSpec((SEQ, RET_V), lambda b, h: (b, v0 + h)),
                  pl.BlockSpec((SEQ, RET_V), lambda b, h: (b, g0 + h)),
                  tab, tab],
        out_specs=pl.BlockSpec((SEQ, RET_V), lambda b, h: (b, h)),
        out_shape=jax.ShapeDtypeStruct((TOKENS, RET_V_WIDTH), BF16),
        scratch_shapes=[pltpu.VMEM((RET_QK, RET_V), F32)],
        compiler_params=_cparams(2, 48),
        name="retention",
    )(log_gamma, proj, proj, proj, proj, cos_r, sin_r)


ROUTER_TM = 512


def _router_kernel(x_ref, g_ref, sc_ref, sh_ref, w_ref, b_ref, ei_ref, wcol_ref, cnt_ref, hp_ref,
                   carry_ref):
    tm = ROUTER_TM

    @pl.when(pl.program_id(0) == 0)
    def _():
        carry_ref[...] = jnp.zeros_like(carry_ref)

    def split(a):
        hi = a.astype(BF16)
        return hi, (a - hi.astype(F32)).astype(BF16)

    h = _norm_mod(x_ref[...], g_ref[...], sc_ref[0], sh_ref[0])
    _to_row_tiles(hp_ref, h, tm)
    h_hi, h_lo = split(h)
    w_hi, w_lo = split(w_ref[...])
    nt = lambda a, b: lax.dot_general(a, b, NT_DIMS, preferred_element_type=F32)
    logits = (nt(w_hi, h_hi) + nt(w_hi, h_lo) + nt(w_lo, h_hi)) + b_ref[...]
    e_log = logits[0:N_EXPERTS]
    g_log = logits[N_EXPERTS:N_EXPERTS + N_GROUPS]

    def top1(vals, n):
        rows = lax.broadcasted_iota(I32, (n, tm), 0)
        m = jnp.max(vals, axis=0, keepdims=True)
        return m, jnp.min(jnp.where(vals == m, rows, n), axis=0, keepdims=True), rows

    g_max, g_idx, _ = top1(g_log, N_GROUPS)
    g_w = 1.0 / jnp.sum(jnp.exp(g_log - g_max), axis=0, keepdims=True)
    sel = e_log[0:EXPERTS_PER_GROUP]
    for gi in range(1, N_GROUPS):
        sel = jnp.where(g_idx == gi, e_log[gi * EXPERTS_PER_GROUP:(gi + 1) * EXPERTS_PER_GROUP], sel)
    m1, i1, rows8 = top1(sel, EXPERTS_PER_GROUP)
    m2, i2, _ = top1(jnp.where(rows8 == i1, -jnp.inf, sel), EXPERTS_PER_GROUP)
    ratio = jnp.exp(m2 - m1)
    w1 = g_w / (1.0 + ratio)
    w2 = (g_w * ratio) / (1.0 + ratio)
    e1 = g_idx * EXPERTS_PER_GROUP + i1
    e2 = g_idx * EXPERTS_PER_GROUP + i2

    rows32 = lax.broadcasted_iota(I32, (N_EXPERTS, tm), 0)
    hit1 = rows32 == e1
    hit2 = rows32 == e2
    onehot = jnp.where(hit1 | hit2, 1.0, 0.0)
    jj = lax.broadcasted_iota(I32, (tm, tm), 0)
    tt = lax.broadcasted_iota(I32, (tm, tm), 1)
    before = (jj < tt).astype(BF16)
    rank_e = jnp.dot(onehot.astype(BF16), before, preferred_element_type=F32) + carry_ref[:, 0:1]
    r1 = jnp.sum(jnp.where(hit1, rank_e, 0.0), axis=0, keepdims=True)
    r2 = jnp.sum(jnp.where(hit2, rank_e, 0.0), axis=0, keepdims=True)
    carry_ref[...] = carry_ref[...] + jnp.sum(onehot, axis=1, keepdims=True)
    cnt_ref[...] = carry_ref[...]

    ei_ref[...] = jnp.where(rows8 == 0, e1, jnp.where(rows8 == 1, e2, jnp.where(
        rows8 == 2, r1.astype(I32), jnp.where(rows8 == 3, r2.astype(I32), 0))))
    rows128 = lax.broadcasted_iota(I32, (LANES, tm), 0)
    wrows = jnp.where(rows128 == 0, w1, jnp.where(rows128 == 1, w2, 0.0))
    wcol_ref[...] = jnp.transpose(wrows)


def _router(x2d, g, scale, shift, w_t, b_rows):
    tm = ROUTER_TM
    per_batch = SEQ // tm
    mod = pl.BlockSpec((1, 1, D_MODEL), lambda i: (i // per_batch, 0, 0))
    return pl.pallas_call(
        _router_kernel,
        grid=(TOKENS // tm,),
        in_specs=[pl.BlockSpec((tm, D_MODEL), lambda i: (i, 0)),
                  pl.BlockSpec((1, D_MODEL), lambda i: (0, 0)),
                  mod, mod,
                  pl.BlockSpec((ROUTER_ROWS, D_MODEL), lambda i: (0, 0)),
                  pl.BlockSpec((ROUTER_ROWS, tm), lambda i: (0, 0))],
        out_specs=[pl.BlockSpec((SUBLANES, tm), lambda i: (0, i)),
                   pl.BlockSpec((tm, LANES), lambda i: (i, 0)),
                   pl.BlockSpec((N_EXPERTS, LANES), lambda i: (0, 0)),
                   pl.BlockSpec((tm * SUBLANES, LANES), lambda i: (i, 0))],
        out_shape=[jax.ShapeDtypeStruct((SUBLANES, TOKENS), I32),
                   jax.ShapeDtypeStruct((TOKENS, LANES), F32),
                   jax.ShapeDtypeStruct((N_EXPERTS, LANES), F32),
                   jax.ShapeDtypeStruct((TOKENS * SUBLANES, LANES), U32)],
        scratch_shapes=[pltpu.VMEM((N_EXPERTS, LANES), F32)],
        compiler_params=_cparams(1, 48),
        name="moe_router",
    )(x2d, g, scale, shift, w_t, b_rows)


DISPATCH_CHUNK = 512
WEIGHT_DMA_PRIORITY = 1


def _row_copy(src, dst, sem):
    return pltpu.make_async_copy(src, dst, sem)


def _dispatch_kernel(pos0_ref, pos1_ref, meta_ref, hp_ref, xs_ref, zero_ref, sem, zsem):
    ch = DISPATCH_CHUNK
    i = pl.program_id(0)
    n = pl.num_programs(0)

    def clear_padding(act):
        def zero_copy(row, nrows):
            return _row_copy(zero_ref.at[pl.ds(0, nrows * SUBLANES), :],
                             xs_ref.at[pl.ds(pl.multiple_of(row * SUBLANES, SUBLANES), nrows * SUBLANES), :],
                             zsem)

        def per_expert(e, carry):
            cnt = meta_ref[e]
            row = meta_ref[N_EXPERTS + e] + cnt
            pad = (-cnt) & (MOE_TM - 1)
            for bit in (128, 64, 32, 16, 8, 4, 2, 1):
                @pl.when((pad & bit) != 0)
                def _(row=row, bit=bit):
                    act(zero_copy(row, bit))
                row = row + (pad & bit)
            return carry

        lax.fori_loop(0, N_EXPERTS, per_expert, 0)

        def per_tile(tl, carry):
            for part in range(MOE_TM // ZERO_ROWS):
                act(zero_copy(tl * MOE_TM + part * ZERO_ROWS, ZERO_ROWS))
            return carry

        lax.fori_loop(meta_ref[2 * N_EXPERTS], MOE_TILES, per_tile, 0)

    @pl.when(i == 0)
    def _():
        zero_ref[...] = jnp.zeros_like(zero_ref)
        clear_padding(lambda cp: cp.start())

    def issue(r, carry):
        tok = i * ch + r
        src = hp_ref.at[pl.ds(pl.multiple_of(tok * SUBLANES, SUBLANES), SUBLANES), :]
        for k, pos_ref in enumerate((pos0_ref, pos1_ref)):
            row = pl.multiple_of(pos_ref[tok] * SUBLANES, SUBLANES)
            _row_copy(src, xs_ref.at[pl.ds(row, SUBLANES), :], sem.at[k]).start(priority=k)
        return carry

    lax.fori_loop(0, ch, issue, 0, unroll=8)

    def wait_chunk():
        for k in range(2):
            _row_copy(hp_ref.at[pl.ds(0, ch * SUBLANES), :], xs_ref.at[pl.ds(0, ch * SUBLANES), :],
                      sem.at[k]).wait()

    @pl.when(i > 0)
    def _():
        wait_chunk()

    @pl.when(i == n - 1)
    def _():
        wait_chunk()
        clear_padding(lambda cp: cp.wait())


def _dispatch(pos0, pos1, meta, h_rows):
    any_spec = pl.BlockSpec(memory_space=pl.ANY)
    grid_spec = pltpu.PrefetchScalarGridSpec(
        num_scalar_prefetch=3,
        grid=(TOKENS // DISPATCH_CHUNK,),
        in_specs=[any_spec],
        out_specs=any_spec,
        scratch_shapes=[pltpu.VMEM((ZERO_ROWS * SUBLANES, LANES), U32),
                        pltpu.SemaphoreType.DMA((2,)),
                        pltpu.SemaphoreType.DMA(())])
    return pl.pallas_call(
        _dispatch_kernel,
        grid_spec=grid_spec,
        out_shape=jax.ShapeDtypeStruct((MOE_ROWS * SUBLANES, LANES), U32),
        compiler_params=_cparams(1, 16),
        name="moe_dispatch",
    )(pos0, pos1, meta, h_rows)


def _experts_kernel(te_ref, tf_ref, tv_ref, ti_ref, nx_ref, ws_ref,
                    xs_ref, wg_hbm, wu_hbm, wd_hbm, ys_ref,
                    wgf, wuf, wdf, wgb, wub, wdb, wsem, *, layer):
    del ti_ref
    tm = MOE_TM
    j = pl.program_id(0)

    def weight_copies(e, s):
        return [_row_copy(wg_hbm.at[layer, e], wgf.at[s], wsem.at[s]),
                _row_copy(wu_hbm.at[layer, e], wuf.at[s], wsem.at[s]),
                _row_copy(wd_hbm.at[layer, e], wdf.at[s], wsem.at[s])]

    @pl.when(j == 0)
    def _():
        for cp in weight_copies(te_ref[0], 0):
            cp.start(priority=WEIGHT_DMA_PRIORITY)

    @pl.when((tv_ref[j] == 1) & (tf_ref[j] == 1))
    def _():
        s = ws_ref[j]

        @pl.when(nx_ref[j] >= 0)
        def _():
            for cp in weight_copies(nx_ref[j], 1 - s):
                cp.start(priority=WEIGHT_DMA_PRIORITY)

        for cp in weight_copies(te_ref[j], s):
            cp.wait()
        wgb[...] = wgf[s].astype(BF16)
        wub[...] = wuf[s].astype(BF16)
        wdb[...] = wdf[s].astype(BF16)

    @pl.when(tv_ref[j] == 1)
    def _():
        x = _from_row_tiles(xs_ref, tm).astype(BF16)
        gate = jnp.dot(x, wgb[...], preferred_element_type=F32)
        up = jnp.dot(x, wub[...], preferred_element_type=F32)
        a = ((gate * jax.nn.sigmoid(gate)) * up).astype(BF16)
        _to_row_tiles(ys_ref, jnp.dot(a, wdb[...], preferred_element_type=F32), tm)

    @pl.when(tv_ref[j] == 0)
    def _():
        ys_ref[...] = jnp.zeros_like(ys_ref)


def _experts(plan, xs, w_gate, w_up, w_down, layer):
    tm = MOE_TM
    any_spec = pl.BlockSpec(memory_space=pl.ANY)
    rows = pl.BlockSpec((tm * SUBLANES, LANES), lambda j, te, tf, tv, ti, nx, ws: (ti[j], 0))
    rows_out = pl.BlockSpec((tm * SUBLANES, LANES), lambda j, *_: (j, 0))
    grid_spec = pltpu.PrefetchScalarGridSpec(
        num_scalar_prefetch=6,
        grid=(MOE_TILES,),
        in_specs=[rows, any_spec, any_spec, any_spec],
        out_specs=rows_out,
        scratch_shapes=[pltpu.VMEM((2, D_MODEL, EXPERT_HIDDEN), F32),
                        pltpu.VMEM((2, D_MODEL, EXPERT_HIDDEN), F32),
                        pltpu.VMEM((2, EXPERT_HIDDEN, D_MODEL), F32),
                        pltpu.VMEM((D_MODEL, EXPERT_HIDDEN), BF16),
                        pltpu.VMEM((D_MODEL, EXPERT_HIDDEN), BF16),
                        pltpu.VMEM((EXPERT_HIDDEN, D_MODEL), BF16),
                        pltpu.SemaphoreType.DMA((2,))])
    return pl.pallas_call(
        functools.partial(_experts_kernel, layer=layer),
        grid_spec=grid_spec,
        out_shape=jax.ShapeDtypeStruct((MOE_ROWS * SUBLANES, LANES), U32),
        compiler_params=_cparams(1, 56),
        name="moe_experts",
    )(*plan, xs, w_gate, w_up, w_down)


COMBINE_TM = 256


def _combine_kernel(pos0_ref, pos1_ref, x_ref, gate_ref, wcol_ref, *refs, final):
    if final:
        fg_ref, ys_ref, o_ref, buf_ref, sem = refs
    else:
        ng_ref, nsc_ref, nsh_ref, ys_ref, o_ref, h_ref, buf_ref, sem = refs
    tm = COMBINE_TM
    i = pl.program_id(0)
    n = pl.num_programs(0)
    slot = i % 2

    def issue(step, s):
        def body(r, carry):
            tok = step * tm + r
            dst_rows = pl.ds(pl.multiple_of(r * SUBLANES, SUBLANES), SUBLANES)
            for k, pos_ref in enumerate((pos0_ref, pos1_ref)):
                row = pl.multiple_of(pos_ref[tok] * SUBLANES, SUBLANES)
                _row_copy(ys_ref.at[pl.ds(row, SUBLANES), :], buf_ref.at[s, k, dst_rows, :],
                          sem.at[s]).start(priority=k)
            return carry

        lax.fori_loop(0, tm, body, 0, unroll=8)

    @pl.when(i == 0)
    def _():
        issue(0, 0)

    @pl.when(i + 1 < n)
    def _():
        issue(i + 1, 1 - slot)

    for k in range(2):
        _row_copy(ys_ref.at[pl.ds(0, tm * SUBLANES), :], buf_ref.at[slot, k], sem.at[slot]).wait()

    y0 = _from_row_tiles(buf_ref.at[slot, 0], tm)
    y1 = _from_row_tiles(buf_ref.at[slot, 1], tm)
    out = x_ref[...] + gate_ref[0] * (wcol_ref[:, 0:1] * y0 + wcol_ref[:, 1:2] * y1)
    if final:
        o_ref[...] = (out * lax.rsqrt(jnp.mean(out * out, axis=-1, keepdims=True) + EPS)) * fg_ref[...]
    else:
        o_ref[...] = out
        h_ref[...] = _norm_mod(out, ng_ref[...], nsc_ref[0], nsh_ref[0]).astype(h_ref.dtype)


def _combine(pos0, pos1, x2d, gate, wcol, ys, final_g=None, next_norm=None):
    tm = COMBINE_TM
    per_batch = SEQ // tm
    final = final_g is not None
    row = pl.BlockSpec((tm, D_MODEL), lambda i, p0, p1: (i, 0))
    vec = pl.BlockSpec((1, D_MODEL), lambda i, p0, p1: (0, 0))
    mod = pl.BlockSpec((1, 1, D_MODEL), lambda i, p0, p1: (i // per_batch, 0, 0))
    extra_specs, extra = ([vec], [final_g]) if final else ([vec, mod, mod], list(next_norm))
    out_f32 = jax.ShapeDtypeStruct((TOKENS, D_MODEL), F32)
    grid_spec = pltpu.PrefetchScalarGridSpec(
        num_scalar_prefetch=2,
        grid=(TOKENS // tm,),
        in_specs=[row, mod, pl.BlockSpec((tm, LANES), lambda i, p0, p1: (i, 0))] + extra_specs
        + [pl.BlockSpec(memory_space=pl.ANY)],
        out_specs=row if final else [row, row],
        scratch_shapes=[pltpu.VMEM((2, 2, tm * SUBLANES, LANES), U32),
                        pltpu.SemaphoreType.DMA((2,))])
    return pl.pallas_call(
        functools.partial(_combine_kernel, final=final),
        grid_spec=grid_spec,
        out_shape=out_f32 if final else [out_f32, jax.ShapeDtypeStruct((TOKENS, D_MODEL), BF16)],
        compiler_params=_cparams(1, 48),
        name="moe_combine",
    )(pos0, pos1, x2d, gate, wcol, *extra, ys)


def _moe_plan(counts):
    counts = counts.astype(I32)
    tiles = (counts + (MOE_TM - 1)) // MOE_TM
    cum = jnp.cumsum(tiles)
    first_tile = cum - tiles
    n_used = cum[-1]
    j = jnp.arange(MOE_TILES, dtype=I32)
    te = jnp.minimum(jnp.sum((j[:, None] >= cum[None, :]).astype(I32), axis=1), N_EXPERTS - 1)
    valid = j < n_used
    te = jnp.where(valid, te, jnp.take(te, n_used - 1))
    tf = (valid & (j == jnp.take(first_tile, te))).astype(I32)
    tv = valid.astype(I32)
    ti = jnp.where(valid, j, n_used - 1)
    has_rows = tiles > 0
    ordinal = jnp.cumsum(has_rows.astype(I32)) - 1
    e_ids = jnp.arange(N_EXPERTS, dtype=I32)
    later = (e_ids[None, :] > e_ids[:, None]) & has_rows[None, :]
    succ = jnp.min(jnp.where(later, e_ids[None, :], N_EXPERTS), axis=1)
    succ = jnp.where(succ == N_EXPERTS, -1, succ)
    nx = jnp.take(succ, te)
    ws = jnp.take(ordinal, te) % 2
    row_off = first_tile * MOE_TM
    meta = jnp.concatenate([counts, row_off, n_used.reshape(1)])
    return (te, tf, tv, ti, nx, ws), row_off, meta


def _positions_kernel(ei_ref, off_ref, pos_ref):
    tm = ei_ref.shape[1]
    rows32 = lax.broadcasted_iota(I32, (N_EXPERTS, tm), 0)
    rows8 = lax.broadcasted_iota(I32, (SUBLANES, tm), 0)
    off = off_ref[:, 0:1]

    def pos(k):
        first = jnp.sum(jnp.where(rows32 == ei_ref[k:k + 1, :], off, 0), axis=0, keepdims=True)
        return first + ei_ref[k + 2:k + 3, :]

    pos_ref[...] = jnp.where(rows8 == 0, pos(0), jnp.where(rows8 == 1, pos(1), 0))


def _positions(ei, row_off):
    tm = 2048
    blk = pl.BlockSpec((SUBLANES, tm), lambda i: (0, i))
    return pl.pallas_call(
        _positions_kernel,
        grid=(TOKENS // tm,),
        in_specs=[blk, pl.BlockSpec((N_EXPERTS, LANES), lambda i: (0, 0))],
        out_specs=blk,
        out_shape=jax.ShapeDtypeStruct((SUBLANES, TOKENS), I32),
        compiler_params=_cparams(1, 32),
        name="moe_positions",
    )(ei, jnp.broadcast_to(row_off[:, None], (N_EXPERTS, LANES)))


def _moe_layer(x2d, layer, mods, norm_g, w_group, b_group, w_expert, b_expert, w_gate, w_up, w_down,
               final_g=None, next_norm=None):
    shift, scale, gate = mods
    w_t = jnp.concatenate([w_expert.T, w_group.T,
                           jnp.zeros((ROUTER_ROWS - N_EXPERTS - N_GROUPS, D_MODEL), F32)], axis=0)
    b_rows = jnp.concatenate([b_expert, b_group, jnp.zeros((ROUTER_ROWS - N_EXPERTS - N_GROUPS,), F32)])
    b_rows = jnp.broadcast_to(b_rows[:, None], (ROUTER_ROWS, ROUTER_TM))
    ei, wcol, cnt, h_rows = _router(x2d, norm_g, scale, shift, w_t, b_rows)
    plan, row_off, meta = _moe_plan(cnt[:, 0])
    pos = _positions(ei, row_off)
    pos0, pos1 = pos[0], pos[1]
    xs = _dispatch(pos0, pos1, meta, h_rows)
    ys = _experts(plan, xs, w_gate, w_up, w_down, layer)
    return _combine(pos0, pos1, x2d, gate, wcol, ys, final_g=final_g, next_norm=next_norm)


def _split_mods(mod, layer):
    m = mod[layer]
    return tuple(m[:, k * D_MODEL:(k + 1) * D_MODEL].reshape(BATCH, 1, D_MODEL) for k in range(3))


def kernel(x, c, positions, w_mod_mix, b_mod_mix, norm_mix, w_mod_ffn, b_mod_ffn, norm_ffn, ev_w_in, ev_q_norm, ev_w_q_up, ev_kv_norm, ev_w_kv_up, ev_w_out, od_w_in, od_w_out, moe_w_group, moe_b_group, moe_w_expert, moe_b_expert, moe_w_gate, moe_w_up, moe_w_down, final_norm):
    x2d = x.reshape(TOKENS, D_MODEL)
    c_lanes = jnp.broadcast_to(c[:, :, None], (BATCH, D_MODEL, LANES))
    mod_mix = _mods(c_lanes, w_mod_mix, b_mod_mix)
    mod_ffn = _mods(c_lanes, w_mod_ffn, b_mod_ffn)
    pos_lanes = jnp.broadcast_to(positions.reshape(TOKENS, 1), (TOKENS, LANES))
    cos_m, sin_m, cos_r, sin_r = _rope_tables(pos_lanes)
    final_g = final_norm.reshape(1, D_MODEL)

    shift, scale, gate = _split_mods(mod_mix, 0)
    half = MLA_ROPE // 2
    w_in = ev_w_in[0]
    c_kr0 = 3 * SB_WIDTH + MLA_Q_LORA + MLA_KV_LORA
    zc = lambda n: jnp.zeros((D_MODEL, n), F32)
    w_in_pad = jnp.concatenate([w_in[:, :c_kr0], w_in[:, c_kr0:c_kr0 + half], zc(half),
                                w_in[:, c_kr0 + half:], zc(half), zc(LANES)], axis=1).astype(BF16)
    h = _norm_mod_rows(x2d, norm_mix[0].reshape(1, D_MODEL), scale, shift)
    proj = _proj([h], w_in_pad, 1024, 48)
    wq = ev_w_q_up[0].reshape(MLA_Q_LORA, MLA_HEADS, MLA_NOPE + MLA_ROPE)
    zq = jnp.zeros((MLA_Q_LORA, MLA_HEADS, half), F32)
    wq = jnp.concatenate([wq[:, :, :MLA_NOPE], wq[:, :, MLA_NOPE:MLA_NOPE + half], zq,
                          wq[:, :, MLA_NOPE + half:], zq], axis=2).reshape(MLA_Q_LORA, MLA_HEADS * MLA_QK_PAD)
    wkv = ev_w_kv_up[0].reshape(MLA_KV_LORA, MLA_HEADS, MLA_NOPE + MLA_V)
    wkv = jnp.concatenate([wkv[:, :, :MLA_NOPE].reshape(MLA_KV_LORA, MLA_HEADS * MLA_NOPE),
                           wkv[:, :, MLA_NOPE:].reshape(MLA_KV_LORA, MLA_HEADS * MLA_V)], axis=1)
    q_cat, k_cat, v_mla = _mla_prep(proj, ev_q_norm[0].reshape(1, MLA_Q_LORA),
                                    ev_kv_norm[0].reshape(1, MLA_KV_LORA), wq, wkv, cos_m, sin_m)
    o_sb = _sb_attention(proj)
    o_mla = _mla_attention(q_cat, k_cat, v_mla)
    x2d = _proj([o_sb, o_mla], ev_w_out[0], 512, 40, x2d, gate)
    shift, scale, gate = _split_mods(mod_mix, 1)
    x2d, h = _moe_layer(x2d, 0, _split_mods(mod_ffn, 0), norm_ffn[0].reshape(1, D_MODEL),
                        moe_w_group[0], moe_b_group[0], moe_w_expert[0], moe_b_expert[0],
                        moe_w_gate, moe_w_up, moe_w_down,
                        next_norm=(norm_mix[1].reshape(1, D_MODEL), scale, shift))

    proj = _proj([h], od_w_in[0], 1024, 48)
    o_ret = _retention(proj, cos_r, sin_r)
    x2d = _proj([o_ret], od_w_out[0], 512, 56, x2d, gate)
    out = _moe_layer(x2d, 1, _split_mods(mod_ffn, 1), norm_ffn[1].reshape(1, D_MODEL),
                     moe_w_group[1], moe_b_group[1], moe_w_expert[1], moe_b_expert[1],
                     moe_w_gate, moe_w_up, moe_w_down, final_g=final_g)
    return out.reshape(BATCH, SEQ, D_MODEL)
```
